```python
import jax, jax.numpy as jnp
from jax import lax
import numpy as np

D_MODEL = 1024
BATCH = 32
SEQ = 2048
DEPTH = 1
DEC_BATCH = 32
DEC_SEQ = 64
PAST_LEN = 1024

CHUNK = 64
N_META = 16
D_MIX = D_MODEL
D_GLA = D_MIX // 2
N_GLA_HEADS = 4
GLA_DV = D_GLA // N_GLA_HEADS
GLA_DK = GLA_DV // 2
D_GLA_K = N_GLA_HEADS * GLA_DK
GATE_RANK = 16
GATE_TAU = 16.0
D_CONV = D_MIX - D_GLA
CONV_W = 3
D_FF = 2816
EPS = 1e-6

SPLIT_POINTS = (
    D_GLA_K,
    2 * D_GLA_K,
    2 * D_GLA_K + D_GLA,
    2 * D_GLA_K + 2 * D_GLA,
    2 * D_GLA_K + 2 * D_GLA + GATE_RANK,
    2 * D_GLA_K + 2 * D_GLA + GATE_RANK + D_CONV,
    2 * D_GLA_K + 2 * D_GLA + GATE_RANK + 2 * D_CONV,
)
D_IN = 2 * D_GLA_K + 2 * D_GLA + GATE_RANK + 3 * D_CONV

kernel_name = 'hymba_gla_shortconv_macaron_stream_step'


def rms_norm(x, g):
    xf = x.astype(jnp.float32)
    y = xf * lax.rsqrt(jnp.mean(xf * xf, axis=-1, keepdims=True) + EPS)
    return (y * g.astype(jnp.float32)).astype(x.dtype)


def swiglu_ffn(x, w_up, w_down):
    gate, up = jnp.split(x @ w_up, 2, axis=-1)
    return (jax.nn.silu(gate) * up) @ w_down


def gla_blocked(q, k, v, log_a, s0, block):
    b, l, h, _ = q.shape
    dv = v.shape[-1]
    n = l // block

    def to_blocks(t):
        return jnp.moveaxis(t.reshape(b, n, block, h, t.shape[-1]), 1, 0)

    causal = jnp.tril(jnp.ones((block, block), dtype=bool))[None, :, :, None, None]

    def step(s, inp):
        qc, kc, vc, lc = inp
        g = jnp.cumsum(lc, axis=1)
        g_last = g[:, -1]
        diff = jnp.where(causal, g[:, :, None] - g[:, None, :], -jnp.inf)
        att = jnp.einsum('bihd,bjhd,bijhd->bijh', qc, kc, jnp.exp(diff))
        o_intra = jnp.einsum('bijh,bjhv->bihv', att, vc)
        o_inter = jnp.einsum('bihk,bhkv->bihv', qc * jnp.exp(g), s)
        s_new = jnp.exp(g_last)[..., None] * s + jnp.einsum(
            'bjhk,bjhv->bhkv', kc * jnp.exp(g_last[:, None] - g), vc)
        return s_new, o_intra + o_inter

    s_fin, o = lax.scan(step, s0, (to_blocks(q), to_blocks(k), to_blocks(v), to_blocks(log_a)))
    return jnp.moveaxis(o, 0, 1).reshape(b, l, h, dv), s_fin


def token_mix(h, s_gla, conv_buf, w_in, w_a2, b_a, g_head, w_conv, w_out):
    b, l, _ = h.shape
    f32 = jnp.float32
    q, k, v, r, a_lr, gate_b, gate_c, hc = jnp.split(h @ w_in, SPLIT_POINTS, axis=-1)

    log_a = jax.nn.log_sigmoid((a_lr @ w_a2 + b_a).astype(f32)) / GATE_TAU
    qh = q.astype(f32).reshape(b, l, N_GLA_HEADS, GLA_DK) * GLA_DK ** -0.5
    kh = k.astype(f32).reshape(b, l, N_GLA_HEADS, GLA_DK)
    vh = v.astype(f32).reshape(b, l, N_GLA_HEADS, GLA_DV)
    lah = log_a.reshape(b, l, N_GLA_HEADS, GLA_DK)
    pad = (-l) % CHUNK
    front = lambda t: jnp.pad(t, ((0, 0), (pad, 0), (0, 0), (0, 0)))
    o, s_new = gla_blocked(front(qh), front(kh), front(vh), front(lah), s_gla.astype(f32), CHUNK)
    o = o[:, pad:]
    o = o * lax.rsqrt(jnp.mean(o * o, axis=-1, keepdims=True) + EPS) * g_head.astype(f32)
    o_gla = (o.reshape(b, l, D_GLA) * jax.nn.silu(r.astype(f32))).astype(h.dtype)

    u = gate_c * hc
    u_ext = jnp.concatenate([conv_buf.astype(u.dtype), u], axis=1)
    conv = sum(w_conv[i] * u_ext[:, i:i + l] for i in range(CONV_W))
    o_conv = gate_b * conv
    new_buf = u_ext[:, -(CONV_W - 1):]

    y = jnp.concatenate([o_gla, o_conv], axis=-1) @ w_out
    return y, s_new, new_buf


def layer(x, s_gla, conv_buf, n1, wu1, wd1, n2, w_in, w_a2, b_a, g_head, w_conv, w_out, n3, wu2, wd2):
    x = x + 0.5 * swiglu_ffn(rms_norm(x, n1), wu1, wd1)
    y, s_new, c_new = token_mix(rms_norm(x, n2), s_gla, conv_buf, w_in, w_a2, b_a, g_head, w_conv, w_out)
    x = x + y
    x = x + 0.5 * swiglu_ffn(rms_norm(x, n3), wu2, wd2)
    return x, s_new, c_new


def setup_inputs(seed: int = 0) -> dict:
    key = jax.random.key(seed)
    ks = jax.random.split(key, 20)
    f32 = jnp.float32
    nrm = lambda kk, shape, scale: jax.random.normal(kk, shape, f32) * scale
    gain = lambda kk, shape: 1.0 + 0.01 * jax.random.normal(kk, shape, f32)
    return {
        'x_prompt': nrm(ks[0], (BATCH, SEQ, D_MODEL), 1.0),
        'x_sample': nrm(ks[1], (DEC_BATCH, DEC_SEQ, D_MODEL), 1.0),
        'state_gla': nrm(ks[2], (DEPTH, DEC_BATCH, N_GLA_HEADS, GLA_DK, GLA_DV), 1.0),
        'cache_conv': nrm(ks[3], (DEPTH, DEC_BATCH, CONV_W - 1, D_CONV), 1.0),
        'meta': nrm(ks[4], (N_META, D_MODEL), 1.0),
        'norm_ffn1': gain(ks[5], (DEPTH, D_MODEL)),
        'w_up1': nrm(ks[6], (DEPTH, D_MODEL, 2 * D_FF), D_MODEL ** -0.5),
        'w_down1': nrm(ks[7], (DEPTH, D_FF, D_MODEL), D_FF ** -0.5),
        'norm_mix': gain(ks[8], (DEPTH, D_MODEL)),
        'w_in': nrm(ks[9], (DEPTH, D_MODEL, D_IN), D_MODEL ** -0.5),
        'w_a2': nrm(ks[10], (DEPTH, GATE_RANK, D_GLA_K), GATE_RANK ** -0.5),
        'b_a': nrm(ks[11], (DEPTH, D_GLA_K), 0.1),
        'g_head': gain(ks[12], (DEPTH, GLA_DV)),
        'w_conv': nrm(ks[13], (DEPTH, CONV_W, D_CONV), CONV_W ** -0.5),
        'w_out': nrm(ks[14], (DEPTH, D_MIX, D_MODEL), D_MIX ** -0.5),
        'norm_ffn2': gain(ks[15], (DEPTH, D_MODEL)),
        'w_up2': nrm(ks[16], (DEPTH, D_MODEL, 2 * D_FF), D_MODEL ** -0.5),
        'w_down2': nrm(ks[17], (DEPTH, D_FF, D_MODEL), D_FF ** -0.5),
        'norm_final': gain(ks[18], (D_MODEL,)),
    }


def reference(x_prompt, x_sample, state_gla, cache_conv, meta, norm_ffn1, w_up1, w_down1,
              norm_mix, w_in, w_a2, b_a, g_head, w_conv, w_out, norm_ffn2, w_up2, w_down2,
              norm_final):
    bp = x_prompt.shape[0]
    meta_b = jnp.broadcast_to(meta.astype(x_prompt.dtype)[None], (bp, N_META, D_MODEL))
    xp = jnp.concatenate([meta_b, x_prompt], axis=1)
    xs = x_sample
    s0_p = jnp.zeros((bp, N_GLA_HEADS, GLA_DK, GLA_DV), jnp.float32)
    c0_p = jnp.zeros((bp, CONV_W - 1, D_CONV), x_prompt.dtype)
    sp_l, cp_l, ss_l, cs_l = [], [], [], []
    for i in range(DEPTH):
        lw = (norm_ffn1[i], w_up1[i], w_down1[i], norm_mix[i], w_in[i], w_a2[i], b_a[i],
              g_head[i], w_conv[i], w_out[i], norm_ffn2[i], w_up2[i], w_down2[i])
        xp, sp, cp = layer(xp, s0_p, c0_p, *lw)
        xs, ss, cs = layer(xs, state_gla[i], cache_conv[i], *lw)
        sp_l.append(sp.astype(state_gla.dtype))
        cp_l.append(cp.astype(cache_conv.dtype))
        ss_l.append(ss.astype(state_gla.dtype))
        cs_l.append(cs.astype(cache_conv.dtype))
    y_prompt = rms_norm(xp[:, N_META:], norm_final)
    y_sample = rms_norm(xs, norm_final)
    return (y_prompt, y_sample, jnp.stack(sp_l), jnp.stack(cp_l), jnp.stack(ss_l), jnp.stack(cs_l))
```

```python
import functools

import numpy as np
import jax
import jax.numpy as jnp
from jax import lax
from jax.experimental import pallas as pl
from jax.experimental.pallas import tpu as pltpu

EPS = 1e-6
N_HEADS = 4
DK = 64
DV = 128
D_K = N_HEADS * DK
D_GLA = N_HEADS * DV
D_CONV = 512
GATE_RANK = 16
GATE_TAU = 16.0
CONV_W = 3
GLA_CHUNK = 64
FF_CHUNK = 256
LANES = 128
SUBLANES = 8
VMEM_LIMIT_BYTES = 48 * 1024 * 1024

F32 = jnp.float32
BF16 = jnp.bfloat16


def _rms(x, g):
    ms = jnp.mean(x * x, axis=-1, keepdims=True)
    return x * lax.rsqrt(ms + EPS) * g


def _silu(x):
    return x * (1.0 / (1.0 + jnp.exp(-x)))


def _const_spec(shape):
    nd = len(shape)
    return pl.BlockSpec(shape, lambda *_: (0,) * nd, pipeline_mode=pl.Buffered(1))


def _ffn_body(x_ref, n_ref, wup_ref, wdn_ref, nf_ref, o_ref, *, n_chunks, final_norm):
    x = x_ref[...]
    h = _rms(x, n_ref[...]).astype(BF16)

    def step(j, acc):
        gu = jnp.dot(h, wup_ref[j], preferred_element_type=F32)
        a = _silu(gu[:, :FF_CHUNK]) * gu[:, FF_CHUNK:]
        return acc + jnp.dot(a.astype(BF16), wdn_ref[j], preferred_element_type=F32)

    acc = lax.fori_loop(0, n_chunks, step, jnp.zeros(x.shape, F32))
    y = x + 0.5 * acc
    if final_norm:
        y = _rms(y, nf_ref[...])
    o_ref[...] = y


def _ffn(x2d, n, wup_p, wdn_p, nf, *, final_norm, rows):
    m, d = x2d.shape
    n_chunks = wup_p.shape[0]
    rows = min(rows, m)
    assert m % rows == 0
    body = functools.partial(_ffn_body, n_chunks=n_chunks, final_norm=final_norm)
    return pl.pallas_call(
        body,
        grid=(m // rows,),
        in_specs=[
            pl.BlockSpec((rows, d), lambda i: (i, 0)),
            _const_spec(n.shape),
            _const_spec(wup_p.shape),
            _const_spec(wdn_p.shape),
            _const_spec(nf.shape),
        ],
        out_specs=pl.BlockSpec((rows, d), lambda i: (i, 0)),
        out_shape=jax.ShapeDtypeStruct((m, d), x2d.dtype),
        compiler_params=pltpu.CompilerParams(
            dimension_semantics=("parallel",), vmem_limit_bytes=VMEM_LIMIT_BYTES),
        name="ffn_final" if final_norm else "ffn",
    )(x2d, n, wup_p, wdn_p, nf)


def _mix_body(x_ref, s0_ref, c0_ref, n_ref, win_ref, wa2_ref, ba_ref, gh_ref, wc_ref, wout_ref,
              ltri_ref, cmask_ref, o_ref, sout_ref, cout_ref, st_ref, ct_ref, mix_ref,
              *, nb, tt, chunk, bcast_state):
    t = pl.program_id(1)
    rows = nb * tt
    cps = tt // chunk
    hc = N_HEADS * chunk

    @pl.when(t == 0)
    def _load_state():
        for b in range(nb):
            sb = 0 if bcast_state else b
            st_ref[b] = s0_ref[sb].reshape(D_K, DV).T
            ct_ref[b] = c0_ref[sb]

    x = x_ref[...].reshape(rows, x_ref.shape[-1])
    h = _rms(x, n_ref[...]).astype(BF16)
    u = jnp.dot(h, win_ref[...], preferred_element_type=F32)
    q = u[:, 0:D_K]
    k = u[:, D_K:2 * D_K]
    v = u[:, 2 * D_K:2 * D_K + D_GLA]
    r = u[:, 1024:1536]
    gate_b = u[:, 1536:2048]
    gate_c = u[:, 2048:2560]
    hconv = u[:, 2560:3072]
    a_lr = u[:, 3072:3072 + LANES]

    z = jnp.dot(a_lr.astype(BF16), wa2_ref[...], preferred_element_type=F32) + ba_ref[...]
    log_a = -(jnp.maximum(-z, 0.0) + jnp.log1p(jnp.exp(-jnp.abs(z)))) * (1.0 / GATE_TAU)

    la1 = log_a.astype(BF16)
    rem = log_a - la1.astype(F32)
    la2 = rem.astype(BF16)
    la3 = (rem - la2.astype(F32)).astype(BF16)
    ltri = ltri_ref[...]
    g_parts, glast_parts, decay = [], [], []
    for c in range(nb * cps):
        sl = slice(c * chunk, (c + 1) * chunk)
        split = jnp.concatenate([la1[sl], la2[sl], la3[sl]], axis=0)
        g_c = jnp.dot(ltri, split, preferred_element_type=F32)
        gl = g_c[chunk - 1:chunk, :]
        g_parts.append(g_c)
        glast_parts.append(jnp.broadcast_to(gl, (chunk, D_K)))
        decay.append(jnp.exp(gl))
    g = jnp.concatenate(g_parts, axis=0)
    glast = jnp.concatenate(glast_parts, axis=0)

    qg = (q * (jnp.exp(g) * (DK ** -0.5))).astype(BF16)
    kng = (k * jnp.exp(-g)).astype(BF16)
    kgl = (k * jnp.exp(glast - g)).astype(BF16)

    lane_head = lax.broadcasted_iota(jnp.int32, (1, D_K), 1) // DK
    score_head = lax.broadcasted_iota(jnp.int32, (1, hc), 1) // chunk
    zero = jnp.zeros((), BF16)
    cmask = cmask_ref[...]
    g_head = gh_ref[...]

    def head_stack(a, head_of_lane):
        return jnp.concatenate([jnp.where(head_of_lane == hh, a, zero) for hh in range(N_HEADS)], axis=0)

    for b in range(nb):
        s_t = st_ref[b]
        for cc in range(cps):
            c = b * cps + cc
            sl = slice(c * chunk, (c + 1) * chunk)
            qg_c, kng_c, kgl_c = qg[sl], kng[sl], kgl[sl]
            v_c, r_c = v[sl], r[sl]
            x_st = jnp.concatenate([v_c[:, hh * DV:(hh + 1) * DV] for hh in range(N_HEADS)], axis=0)
            r_st = jnp.concatenate([r_c[:, hh * DV:(hh + 1) * DV] for hh in range(N_HEADS)], axis=0)
            x_bf = x_st.astype(BF16)
            att = lax.dot_general(qg_c, head_stack(kng_c, lane_head), (((1,), (1,)), ((), ())),
                                  preferred_element_type=F32)
            att = (att * cmask).astype(BF16)
            o_st = jnp.dot(head_stack(att, score_head), x_bf, preferred_element_type=F32)
            o_st = o_st + lax.dot_general(head_stack(qg_c, lane_head), s_t.astype(BF16),
                                          (((1,), (1,)), ((), ())), preferred_element_type=F32)
            ds = jnp.dot(x_st.T.astype(BF16), head_stack(kgl_c, lane_head), preferred_element_type=F32)
            s_t = s_t * decay[c] + ds
            ms = jnp.mean(o_st * o_st, axis=-1, keepdims=True)
            o_st = o_st * lax.rsqrt(ms + EPS) * g_head * _silu(r_st)
            for hh in range(N_HEADS):
                mix_ref[sl, hh * DV:(hh + 1) * DV] = o_st[hh * chunk:(hh + 1) * chunk]
        st_ref[b] = s_t

    uc = gate_c * hconv
    wc = wc_ref[...]
    row = lax.broadcasted_iota(jnp.int32, (tt, D_CONV), 0)
    for b in range(nb):
        ub = uc[b * tt:(b + 1) * tt]
        tail = ct_ref[b]
        t1 = jnp.broadcast_to(tail[SUBLANES - 1:SUBLANES], (tt, D_CONV))
        t2 = jnp.broadcast_to(tail[SUBLANES - 2:SUBLANES - 1], (tt, D_CONV))
        u1 = jnp.where(row == 0, t1, pltpu.roll(ub, 1, 0))
        u2 = jnp.where(row == 0, t2, jnp.where(row == 1, t1, pltpu.roll(ub, 2, 0)))
        conv = wc[0:1] * u2 + wc[1:2] * u1 + wc[2:3] * ub
        mix_ref[b * tt:(b + 1) * tt, D_GLA:D_GLA + D_CONV] = gate_b[b * tt:(b + 1) * tt] * conv
        ct_ref[b] = ub[tt - SUBLANES:tt]

    y = jnp.dot(mix_ref[...].astype(BF16), wout_ref[...], preferred_element_type=F32)
    o_ref[...] = (x + y).reshape(o_ref.shape)

    @pl.when(t == pl.num_programs(1) - 1)
    def _store_state():
        for b in range(nb):
            sout_ref[b] = st_ref[b].T.reshape(N_HEADS, DK, DV)
            cout_ref[b] = ct_ref[b]


def _mix(x, s0, c0, n, win_p, wa2_p, ba, gh, wc, wout, *, nb, tt):
    bsz, seq, d = x.shape
    nb = min(nb, bsz)
    tt = min(tt, seq)
    assert bsz % nb == 0 and seq % tt == 0
    chunk = min(GLA_CHUNK, tt)
    assert tt % chunk == 0
    bcast_state = s0.shape[0] != bsz
    assert not bcast_state or (s0.shape[0] == 1 and c0.shape[0] == 1)
    nbs = 1 if bcast_state else nb
    state_idx = (lambda b, t: (0, 0, 0, 0)) if bcast_state else (lambda b, t: (b, 0, 0, 0))
    tail_idx = (lambda b, t: (0, 0, 0)) if bcast_state else (lambda b, t: (b, 0, 0))

    tri = np.tril(np.ones((chunk, chunk), np.float32))
    ltri = jnp.asarray(np.concatenate([tri, tri, tri], axis=1), dtype=BF16)
    cmask = jnp.asarray(np.tile(tri, (1, N_HEADS)), dtype=F32)

    body = functools.partial(_mix_body, nb=nb, tt=tt, chunk=chunk, bcast_state=bcast_state)
    return pl.pallas_call(
        body,
        grid=(bsz // nb, seq // tt),
        in_specs=[
            pl.BlockSpec((nb, tt, d), lambda b, t: (b, t, 0)),
            pl.BlockSpec((nbs, N_HEADS, DK, DV), state_idx),
            pl.BlockSpec((nbs, SUBLANES, D_CONV), tail_idx),
            _const_spec(n.shape),
            _const_spec(win_p.shape),
            _const_spec(wa2_p.shape),
            _const_spec(ba.shape),
            _const_spec(gh.shape),
            _const_spec(wc.shape),
            _const_spec(wout.shape),
            _const_spec(ltri.shape),
            _const_spec(cmask.shape),
        ],
        out_specs=[
            pl.BlockSpec((nb, tt, d), lambda b, t: (b, t, 0)),
            pl.BlockSpec((nb, N_HEADS, DK, DV), lambda b, t: (b, 0, 0, 0)),
            pl.BlockSpec((nb, SUBLANES, D_CONV), lambda b, t: (b, 0, 0)),
        ],
        out_shape=[
            jax.ShapeDtypeStruct((bsz, seq, d), x.dtype),
            jax.ShapeDtypeStruct((bsz, N_HEADS, DK, DV), F32),
            jax.ShapeDtypeStruct((bsz, SUBLANES, D_CONV), F32),
        ],
        scratch_shapes=[
            pltpu.VMEM((nb, DV, D_K), F32),
            pltpu.VMEM((nb, SUBLANES, D_CONV), F32),
            pltpu.VMEM((nb * tt, D_GLA + D_CONV), F32),
        ],
        compiler_params=pltpu.CompilerParams(
            dimension_semantics=("parallel", "arbitrary"), vmem_limit_bytes=VMEM_LIMIT_BYTES),
        name="mix",
    )(x, s0, c0, n, win_p, wa2_p, ba, gh, wc, wout, ltri, cmask)


def _pack_ffn(w_up, w_down):
    d, two_f = w_up.shape
    f = two_f // 2
    assert f % FF_CHUNK == 0
    nch = f // FF_CHUNK
    gate = w_up[:, :f].reshape(d, nch, FF_CHUNK)
    up = w_up[:, f:].reshape(d, nch, FF_CHUNK)
    wup_p = jnp.concatenate([gate, up], axis=-1).transpose(1, 0, 2).astype(BF16)
    wdn_p = w_down.reshape(nch, FF_CHUNK, d).astype(BF16)
    return wup_p, wdn_p


def _pack_mix(w_in, w_a2):
    d = w_in.shape[0]
    lo = 2 * D_K + 2 * D_GLA
    hi = lo + GATE_RANK
    pad = jnp.zeros((d, LANES - GATE_RANK), w_in.dtype)
    win_p = jnp.concatenate([w_in[:, :lo], w_in[:, hi:], w_in[:, lo:hi], pad], axis=1).astype(BF16)
    wa2_p = jnp.concatenate([w_a2, jnp.zeros((LANES - GATE_RANK, w_a2.shape[1]), w_a2.dtype)], axis=0).astype(BF16)
    return win_p, wa2_p


def kernel(x_prompt, x_sample, state_gla, cache_conv, meta, norm_ffn1, w_up1, w_down1, norm_mix, w_in, w_a2, b_a, g_head, w_conv, w_out, norm_ffn2, w_up2, w_down2, norm_final):
    bp, seq, d = x_prompt.shape
    bs, seq_s, _ = x_sample.shape
    depth = w_in.shape[0]
    dt = x_prompt.dtype

    xm = meta.astype(dt)[None]
    xp, xs = x_prompt, x_sample
    sm = jnp.zeros((1, N_HEADS, DK, DV), F32)
    cm = jnp.zeros((1, SUBLANES, D_CONV), dt)
    nf = norm_final.reshape(1, d)
    sp_l, cp_l, ss_l, cs_l = [], [], [], []
    for i in range(depth):
        last = i == depth - 1
        wup1_p, wdn1_p = _pack_ffn(w_up1[i], w_down1[i])
        wup2_p, wdn2_p = _pack_ffn(w_up2[i], w_down2[i])
        win_p, wa2_p = _pack_mix(w_in[i], w_a2[i])
        mixw = (norm_mix[i].reshape(1, d), win_p, wa2_p, b_a[i].reshape(1, D_K), g_head[i].reshape(1, DV),
                w_conv[i], w_out[i].astype(BF16))
        n1 = norm_ffn1[i].reshape(1, d)
        n3 = norm_ffn2[i].reshape(1, d)
        cs0 = jnp.pad(cache_conv[i].astype(dt), ((0, 0), (SUBLANES - (CONV_W - 1), 0), (0, 0)))

        def ffn1(x):
            return _ffn(x.reshape(-1, d), n1, wup1_p, wdn1_p, nf, final_norm=False, rows=512).reshape(x.shape)

        def ffn2(x):
            return _ffn(x.reshape(-1, d), n3, wup2_p, wdn2_p, nf, final_norm=last, rows=512).reshape(x.shape)

        xm, sm_new, cm_new = _mix(ffn1(xm), sm, cm, *mixw, nb=1, tt=meta.shape[0])
        xp, sp, cp = _mix(ffn1(xp), sm_new, cm_new, *mixw, nb=1, tt=512)
        xs, ss, cs = _mix(ffn1(xs), state_gla[i].astype(F32), cs0, *mixw, nb=8, tt=seq_s)
        xp = ffn2(xp)
        xs = ffn2(xs)
        if not last:
            xm = ffn2(xm)
        sp_l.append(sp.astype(state_gla.dtype))
        cp_l.append(cp[:, SUBLANES - (CONV_W - 1):].astype(cache_conv.dtype))
        ss_l.append(ss.astype(state_gla.dtype))
        cs_l.append(cs[:, SUBLANES - (CONV_W - 1):].astype(cache_conv.dtype))
    return (xp, xs, jnp.stack(sp_l), jnp.stack(cp_l), jnp.stack(ss_l), jnp.stack(cs_l))
```

```python
import functools

import numpy as np
import jax
import jax.numpy as jnp
from jax import lax
from jax.experimental import pallas as pl
from jax.experimental.pallas import tpu as pltpu

EPS = 1e-6
N_HEADS = 4
DK = 64
DV = 128
D_K = N_HEADS * DK
D_GLA = N_HEADS * DV
D_CONV = 512
GATE_RANK = 16
GATE_TAU = 16.0
CONV_W = 3
GLA_CHUNK = 64
FF_CHUNK = 256
LANES = 128
SUBLANES = 8
VMEM_LIMIT_BYTES = 48 * 1024 * 1024

F32 = jnp.float32
BF16 = jnp.bfloat16


def _rms(x, g):
    ms = jnp.mean(x * x, axis=-1, keepdims=True)
    return x * lax.rsqrt(ms + EPS) * g


def _silu(x):
    return x * (1.0 / (1.0 + jnp.exp(-x)))


def _const_spec(shape):
    nd = len(shape)
    return pl.BlockSpec(shape, lambda *_: (0,) * nd, pipeline_mode=pl.Buffered(1))


def _ffn_body(x_ref, n_ref, wup_ref, wdn_ref, nf_ref, o_ref, *, n_chunks, final_norm):
    x = x_ref[...]
    h = _rms(x, n_ref[...]).astype(BF16)

    acc = None
    for j in range(n_chunks):
        gu = jnp.dot(h, wup_ref[j], preferred_element_type=F32)
        a = _silu(gu[:, :FF_CHUNK]) * gu[:, FF_CHUNK:]
        part = jnp.dot(a.astype(BF16), wdn_ref[j], preferred_element_type=F32)
        acc = part if acc is None else acc + part
    y = x + 0.5 * acc
    if final_norm:
        y = _rms(y, nf_ref[...])
    o_ref[...] = y


def _ffn(x2d, n, wup_p, wdn_p, nf, *, final_norm, rows):
    m, d = x2d.shape
    n_chunks = wup_p.shape[0]
    rows = min(rows, m)
    assert m % rows == 0
    body = functools.partial(_ffn_body, n_chunks=n_chunks, final_norm=final_norm)
    return pl.pallas_call(
        body,
        grid=(m // rows,),
        in_specs=[
            pl.BlockSpec((rows, d), lambda i: (i, 0)),
            _const_spec(n.shape),
            _const_spec(wup_p.shape),
            _const_spec(wdn_p.shape),
            _const_spec(nf.shape),
        ],
        out_specs=pl.BlockSpec((rows, d), lambda i: (i, 0)),
        out_shape=jax.ShapeDtypeStruct((m, d), x2d.dtype),
        compiler_params=pltpu.CompilerParams(
            dimension_semantics=("parallel",), vmem_limit_bytes=VMEM_LIMIT_BYTES),
        name="ffn_final" if final_norm else "ffn",
    )(x2d, n, wup_p, wdn_p, nf)


def _mix_body(x_ref, s0_ref, c0_ref, n_ref, win_ref, wa2_ref, ba_ref, gh_ref, wc_ref, wout_ref,
              ltri_ref, cmask_ref, o_ref, sout_ref, cout_ref, st_ref, ct_ref, mix_ref,
              *, nb, tt, chunk, bcast_state):
    t = pl.program_id(1)
    rows = nb * tt
    cps = tt // chunk
    hc = N_HEADS * chunk

    @pl.when(t == 0)
    def _load_state():
        for b in range(nb):
            sb = 0 if bcast_state else b
            st_ref[b] = s0_ref[sb].reshape(D_K, DV).T
            ct_ref[b] = c0_ref[sb]

    x = x_ref[...].reshape(rows, x_ref.shape[-1])
    h = _rms(x, n_ref[...]).astype(BF16)
    u = jnp.dot(h, win_ref[...], preferred_element_type=F32)
    q = u[:, 0:D_K]
    k = u[:, D_K:2 * D_K]
    v = u[:, 2 * D_K:2 * D_K + D_GLA]
    r = u[:, 1024:1536]
    gate_b = u[:, 1536:2048]
    gate_c = u[:, 2048:2560]
    hconv = u[:, 2560:3072]
    a_lr = u[:, 3072:3072 + LANES]

    z = jnp.dot(a_lr.astype(BF16), wa2_ref[...], preferred_element_type=F32) + ba_ref[...]
    log_a = -(jnp.maximum(-z, 0.0) + jnp.log1p(jnp.exp(-jnp.abs(z)))) * (1.0 / GATE_TAU)

    la1 = log_a.astype(BF16)
    rem = log_a - la1.astype(F32)
    la2 = rem.astype(BF16)
    la3 = (rem - la2.astype(F32)).astype(BF16)
    ltri = ltri_ref[...]
    g_parts, glast_parts, decay = [], [], []
    for c in range(nb * cps):
        sl = slice(c * chunk, (c + 1) * chunk)
        split = jnp.concatenate([la1[sl], la2[sl], la3[sl]], axis=0)
        g_c = jnp.dot(ltri, split, preferred_element_type=F32)
        gl = g_c[chunk - 1:chunk, :]
        g_parts.append(g_c)
        glast_parts.append(jnp.broadcast_to(gl, (chunk, D_K)))
        decay.append(jnp.exp(gl))
    g = jnp.concatenate(g_parts, axis=0)
    glast = jnp.concatenate(glast_parts, axis=0)

    qg = (q * (jnp.exp(g) * (DK ** -0.5))).astype(BF16)
    kng = (k * jnp.exp(-g)).astype(BF16)
    kgl = (k * jnp.exp(glast - g)).astype(BF16)

    lane_head = lax.broadcasted_iota(jnp.int32, (1, D_K), 1) // DK
    score_head = lax.broadcasted_iota(jnp.int32, (1, hc), 1) // chunk
    zero = jnp.zeros((), BF16)
    cmask = cmask_ref[...]
    g_head = gh_ref[...]

    def head_stack(a, head_of_lane):
        return jnp.concatenate([jnp.where(head_of_lane == hh, a, zero) for hh in range(N_HEADS)], axis=0)

    for b in range(nb):
        s_t = st_ref[b]
        for cc in range(cps):
            c = b * cps + cc
            sl = slice(c * chunk, (c + 1) * chunk)
            qg_c, kng_c, kgl_c = qg[sl], kng[sl], kgl[sl]
            v_c, r_c = v[sl], r[sl]
            x_st = jnp.concatenate([v_c[:, hh * DV:(hh + 1) * DV] for hh in range(N_HEADS)], axis=0)
            r_st = jnp.concatenate([r_c[:, hh * DV:(hh + 1) * DV] for hh in range(N_HEADS)], axis=0)
            x_bf = x_st.astype(BF16)
            att = lax.dot_general(qg_c, head_stack(kng_c, lane_head), (((1,), (1,)), ((), ())),
                                  preferred_element_type=F32)
            att = (att * cmask).astype(BF16)
            o_st = jnp.dot(head_stack(att, score_head), x_bf, preferred_element_type=F32)
            o_st = o_st + lax.dot_general(head_stack(qg_c, lane_head), s_t.astype(BF16),
                                          (((1,), (1,)), ((), ())), preferred_element_type=F32)
            ds = jnp.dot(x_st.T.astype(BF16), head_stack(kgl_c, lane_head), preferred_element_type=F32)
            s_t = s_t * decay[c] + ds
            ms = jnp.mean(o_st * o_st, axis=-1, keepdims=True)
            o_st = o_st * lax.rsqrt(ms + EPS) * g_head * _silu(r_st)
            for hh in range(N_HEADS):
                mix_ref[sl, hh * DV:(hh + 1) * DV] = o_st[hh * chunk:(hh + 1) * chunk]
        st_ref[b] = s_t

    uc = gate_c * hconv
    wc = wc_ref[...]
    row = lax.broadcasted_iota(jnp.int32, (tt, D_CONV), 0)
    for b in range(nb):
        ub = uc[b * tt:(b + 1) * tt]
        tail = ct_ref[b]
        t1 = jnp.broadcast_to(tail[SUBLANES - 1:SUBLANES], (tt, D_CONV))
        t2 = jnp.broadcast_to(tail[SUBLANES - 2:SUBLANES - 1], (tt, D_CONV))
        u1 = jnp.where(row == 0, t1, pltpu.roll(ub, 1, 0))
        u2 = jnp.where(row == 0, t2, jnp.where(row == 1, t1, pltpu.roll(ub, 2, 0)))
        conv = wc[0:1] * u2 + wc[1:2] * u1 + wc[2:3] * ub
        mix_ref[b * tt:(b + 1) * tt, D_GLA:D_GLA + D_CONV] = gate_b[b * tt:(b + 1) * tt] * conv
        ct_ref[b] = ub[tt - SUBLANES:tt]

    y = jnp.dot(mix_ref[...].astype(BF16), wout_ref[...], preferred_element_type=F32)
    o_ref[...] = (x + y).reshape(o_ref.shape)

    @pl.when(t == pl.num_programs(1) - 1)
    def _store_state():
        for b in range(nb):
            sout_ref[b] = st_ref[b].T.reshape(N_HEADS, DK, DV)
            cout_ref[b] = ct_ref[b]


def _mix(x, s0, c0, n, win_p, wa2_p, ba, gh, wc, wout, *, nb, tt):
    bsz, seq, d = x.shape
    nb = min(nb, bsz)
    tt = min(tt, seq)
    assert bsz % nb == 0 and seq % tt == 0
    chunk = min(GLA_CHUNK, tt)
    assert tt % chunk == 0
    bcast_state = s0.shape[0] != bsz
    assert not bcast_state or (s0.shape[0] == 1 and c0.shape[0] == 1)
    nbs = 1 if bcast_state else nb
    state_idx = (lambda b, t: (0, 0, 0, 0)) if bcast_state else (lambda b, t: (b, 0, 0, 0))
    tail_idx = (lambda b, t: (0, 0, 0)) if bcast_state else (lambda b, t: (b, 0, 0))

    tri = np.tril(np.ones((chunk, chunk), np.float32))
    ltri = jnp.asarray(np.concatenate([tri, tri, tri], axis=1), dtype=BF16)
    cmask = jnp.asarray(np.tile(tri, (1, N_HEADS)), dtype=F32)

    body = functools.partial(_mix_body, nb=nb, tt=tt, chunk=chunk, bcast_state=bcast_state)
    return pl.pallas_call(
        body,
        grid=(bsz // nb, seq // tt),
        in_specs=[
            pl.BlockSpec((nb, tt, d), lambda b, t: (b, t, 0)),
            pl.BlockSpec((nbs, N_HEADS, DK, DV), state_idx),
            pl.BlockSpec((nbs, SUBLANES, D_CONV), tail_idx),
            _const_spec(n.shape),
            _const_spec(win_p.shape),
            _const_spec(wa2_p.shape),
            _const_spec(ba.shape),
            _const_spec(gh.shape),
            _const_spec(wc.shape),
            _const_spec(wout.shape),
            _const_spec(ltri.shape),
            _const_spec(cmask.shape),
        ],
        out_specs=[
            pl.BlockSpec((nb, tt, d), lambda b, t: (b, t, 0)),
            pl.BlockSpec((nb, N_HEADS, DK, DV), lambda b, t: (b, 0, 0, 0)),
            pl.BlockSpec((nb, SUBLANES, D_CONV), lambda b, t: (b, 0, 0)),
        ],
        out_shape=[
            jax.ShapeDtypeStruct((bsz, seq, d), x.dtype),
            jax.ShapeDtypeStruct((bsz, N_HEADS, DK, DV), F32),
            jax.ShapeDtypeStruct((bsz, SUBLANES, D_CONV), F32),
        ],
        scratch_shapes=[
            pltpu.VMEM((nb, DV, D_K), F32),
            pltpu.VMEM((nb, SUBLANES, D_CONV), F32),
            pltpu.VMEM((nb * tt, D_GLA + D_CONV), F32),
        ],
        compiler_params=pltpu.CompilerParams(
            dimension_semantics=("parallel", "arbitrary"), vmem_limit_bytes=VMEM_LIMIT_BYTES),
        name="mix",
    )(x, s0, c0, n, win_p, wa2_p, ba, gh, wc, wout, ltri, cmask)


def _pack_ffn(w_up, w_down):
    d, two_f = w_up.shape
    f = two_f // 2
    assert f % FF_CHUNK == 0
    nch = f // FF_CHUNK
    gate = w_up[:, :f].reshape(d, nch, FF_CHUNK)
    up = w_up[:, f:].reshape(d, nch, FF_CHUNK)
    wup_p = jnp.concatenate([gate, up], axis=-1).transpose(1, 0, 2).astype(BF16)
    wdn_p = w_down.reshape(nch, FF_CHUNK, d).astype(BF16)
    return wup_p, wdn_p


def _pack_mix(w_in, w_a2):
    d = w_in.shape[0]
    lo = 2 * D_K + 2 * D_GLA
    hi = lo + GATE_RANK
    pad = jnp.zeros((d, LANES - GATE_RANK), w_in.dtype)
    win_p = jnp.concatenate([w_in[:, :lo], w_in[:, hi:], w_in[:, lo:hi], pad], axis=1).astype(BF16)
    wa2_p = jnp.concatenate([w_a2, jnp.zeros((LANES - GATE_RANK, w_a2.shape[1]), w_a2.dtype)], axis=0).astype(BF16)
    return win_p, wa2_p


def kernel(x_prompt, x_sample, state_gla, cache_conv, meta, norm_ffn1, w_up1, w_down1, norm_mix, w_in, w_a2, b_a, g_head, w_conv, w_out, norm_ffn2, w_up2, w_down2, norm_final):
    bp, seq, d = x_prompt.shape
    bs, seq_s, _ = x_sample.shape
    depth = w_in.shape[0]
    dt = x_prompt.dtype

    xm = meta.astype(dt)[None]
    xp, xs = x_prompt, x_sample
    sm = jnp.zeros((1, N_HEADS, DK, DV), F32)
    cm = jnp.zeros((1, SUBLANES, D_CONV), dt)
    nf = norm_final.reshape(1, d)
    sp_l, cp_l, ss_l, cs_l = [], [], [], []
    for i in range(depth):
        last = i == depth - 1
        wup1_p, wdn1_p = _pack_ffn(w_up1[i], w_down1[i])
        wup2_p, wdn2_p = _pack_ffn(w_up2[i], w_down2[i])
        win_p, wa2_p = _pack_mix(w_in[i], w_a2[i])
        mixw = (norm_mix[i].reshape(1, d), win_p, wa2_p, b_a[i].reshape(1, D_K), g_head[i].reshape(1, DV),
                w_conv[i], w_out[i].astype(BF16))
        n1 = norm_ffn1[i].reshape(1, d)
        n3 = norm_ffn2[i].reshape(1, d)
        cs0 = jnp.pad(cache_conv[i].astype(dt), ((0, 0), (SUBLANES - (CONV_W - 1), 0), (0, 0)))

        def ffn1(x):
            return _ffn(x.reshape(-1, d), n1, wup1_p, wdn1_p, nf, final_norm=False, rows=512).reshape(x.shape)

        def ffn2(x):
            return _ffn(x.reshape(-1, d), n3, wup2_p, wdn2_p, nf, final_norm=last, rows=512).reshape(x.shape)

        xm, sm_new, cm_new = _mix(ffn1(xm), sm, cm, *mixw, nb=1, tt=meta.shape[0])
        xp, sp, cp = _mix(ffn1(xp), sm_new, cm_new, *mixw, nb=1, tt=512)
        xs, ss, cs = _mix(ffn1(xs), state_gla[i].astype(F32), cs0, *mixw, nb=8, tt=seq_s)
        xp = ffn2(xp)
        xs = ffn2(xs)
        if not last:
            xm = ffn2(xm)
        sp_l.append(sp.astype(state_gla.dtype))
        cp_l.append(cp[:, SUBLANES - (CONV_W - 1):].astype(cache_conv.dtype))
        ss_l.append(ss.astype(state_gla.dtype))
        cs_l.append(cs[:, SUBLANES - (CONV_W - 1):].astype(cache_conv.dtype))
    return (xp, xs, jnp.stack(sp_l), jnp.stack(cp_l), jnp.stack(ss_l), jnp.stack(cs_l))
```

```python
import functools

import numpy as np
import jax
import jax.numpy as jnp
from jax import lax
from jax.experimental import pallas as pl
from jax.experimental.pallas import tpu as pltpu

EPS = 1e-6
N_HEADS = 4
DK = 64
DV = 128
D_K = N_HEADS * DK
D_GLA = N_HEADS * DV
D_CONV = 512
GATE_RANK = 16
GATE_TAU = 16.0
CONV_W = 3
GLA_CHUNK = 64
FF_CHUNK = 256
MIX_SUB_ROWS = 256
SAFE_LOG_DECAY = -60.0
LANES = 128
SUBLANES = 8
VMEM_LIMIT_BYTES = 48 * 1024 * 1024

F32 = jnp.float32
BF16 = jnp.bfloat16


def _rms(x, g):
    ms = jnp.mean(x * x, axis=-1, keepdims=True)
    return x * lax.rsqrt(ms + EPS) * g


def _silu(x):
    return x * (1.0 / (1.0 + jnp.exp(-x)))


def _const_spec(shape):
    nd = len(shape)
    return pl.BlockSpec(shape, lambda *_: (0,) * nd, pipeline_mode=pl.Buffered(1))


def _ffn_body(x_ref, n_ref, wup_ref, wdn_ref, nf_ref, o_ref, *, final_norm):
    x = x_ref[...]
    h = _rms(x, n_ref[...]).astype(BF16)
    f = wdn_ref.shape[0]

    acc = None
    for c0 in range(0, f, FF_CHUNK):
        gate = jnp.dot(h, wup_ref[:, c0:c0 + FF_CHUNK], preferred_element_type=F32)
        up = jnp.dot(h, wup_ref[:, f + c0:f + c0 + FF_CHUNK], preferred_element_type=F32)
        a = (_silu(gate) * up).astype(BF16)
        part = jnp.dot(a, wdn_ref[c0:c0 + FF_CHUNK, :], preferred_element_type=F32)
        acc = part if acc is None else acc + part
    y = x + 0.5 * acc
    if final_norm:
        y = _rms(y, nf_ref[...])
    o_ref[...] = y


def _ffn(x2d, n, wup_p, wdn_p, nf, *, final_norm, rows):
    m, d = x2d.shape
    rows = min(rows, m)
    assert m % rows == 0 and wdn_p.shape[0] % FF_CHUNK == 0
    body = functools.partial(_ffn_body, final_norm=final_norm)
    return pl.pallas_call(
        body,
        grid=(m // rows,),
        in_specs=[
            pl.BlockSpec((rows, d), lambda i: (i, 0)),
            _const_spec(n.shape),
            _const_spec(wup_p.shape),
            _const_spec(wdn_p.shape),
            _const_spec(nf.shape),
        ],
        out_specs=pl.BlockSpec((rows, d), lambda i: (i, 0)),
        out_shape=jax.ShapeDtypeStruct((m, d), x2d.dtype),
        compiler_params=pltpu.CompilerParams(
            dimension_semantics=("parallel",), vmem_limit_bytes=VMEM_LIMIT_BYTES),
        name="ffn_final" if final_norm else "ffn",
    )(x2d, n, wup_p, wdn_p, nf)


def _head_stack(a, head_of_lane):
    zero = jnp.zeros((), a.dtype)
    return jnp.concatenate([jnp.where(head_of_lane == hh, a, zero) for hh in range(N_HEADS)], axis=0)


def _direct_scores(q_c, k_c, g_c, row_ref, chunk):
    hc = N_HEADS * chunk
    row_ref[0] = k_c
    row_ref[1] = g_c
    lane_head = lax.broadcasted_iota(jnp.int32, (1, D_K), 1) // DK
    score_head = lax.broadcasted_iota(jnp.int32, (1, hc), 1) // chunk
    key_of_lane = lax.broadcasted_iota(jnp.int32, (chunk, hc), 1) % chunk

    def one_key(j, att):
        kj = row_ref[0, pl.ds(j, 1), :]
        gj = row_ref[1, pl.ds(j, 1), :]
        p = q_c * kj * jnp.exp(jnp.minimum(g_c - gj, 0.0))
        col = jnp.zeros((chunk, hc), F32)
        for hh in range(N_HEADS):
            s_h = jnp.sum(jnp.where(lane_head == hh, p, 0.0), axis=-1, keepdims=True)
            col = jnp.where(score_head == hh, s_h, col)
        return jnp.where(key_of_lane == j, col, att)

    return lax.fori_loop(0, chunk, one_key, jnp.zeros((chunk, hc), F32))


_U_COLS = {"qk": (0, 2 * D_K), "v": (2 * D_K, D_GLA), "r": (1024, D_GLA), "gate_b": (1536, D_CONV),
           "gate_c": (2048, D_CONV), "hconv": (2560, D_CONV), "a_lr": (3072, LANES)}
_W_OUT_PIECE = 256


def _mix_tile(x_ref, n_ref, win_ref, wa2_ref, ba_ref, gh_ref, wc_ref, wout_ref, ltri_ref, cmask_ref,
              o_ref, st_ref, ct_ref, stn_ref, ctn_ref, row_ref, *, nb, tt, chunk, sub_rows, direct):
    rows = nb * tt
    n_sub = rows // sub_rows
    cpsub = sub_rows // chunk
    hc = N_HEADS * chunk
    d = x_ref.shape[-1]

    lane_head = lax.broadcasted_iota(jnp.int32, (1, D_K), 1) // DK
    score_head = lax.broadcasted_iota(jnp.int32, (1, hc), 1) // chunk
    ltri = ltri_ref[...]
    cmask = cmask_ref[...]
    g_head = gh_ref[...]
    wc = wc_ref[...]
    norm_g = n_ref[...]
    ba = ba_ref[...]

    state = {b: st_ref[b] for b in range(nb)}
    tail = {b: ct_ref[b] for b in range(nb)}
    g_min = None

    def load_x(s):
        r0 = s * sub_rows
        if tt >= sub_rows:
            b0, t0 = divmod(r0, tt)
            return x_ref[b0, t0:t0 + sub_rows, :]
        return x_ref[r0 // tt:(r0 + sub_rows) // tt].reshape(sub_rows, d)

    def store_y(s, y):
        r0 = s * sub_rows
        if tt >= sub_rows:
            b0, t0 = divmod(r0, tt)
            o_ref[b0, t0:t0 + sub_rows, :] = y
        else:
            o_ref[r0 // tt:(r0 + sub_rows) // tt] = y.reshape(sub_rows // tt, tt, d)

    xs, us, ys = {}, {}, {}
    pending = []

    def queue_in_proj(s):
        xs[s] = load_x(s)
        h = _rms(xs[s], norm_g).astype(BF16)
        us[s] = {}

        def piece(name):
            c0, w = _U_COLS[name]
            us[s][name] = jnp.dot(h, win_ref[:, c0:c0 + w], preferred_element_type=F32)
        pending.extend(functools.partial(piece, name) for name in ("a_lr", "qk", "v", "r", "gate_c", "hconv", "gate_b"))

    def queue_out_proj(s, mixed):
        ys[s] = []

        def piece(j):
            ys[s].append(jnp.dot(mixed, wout_ref[:, j * _W_OUT_PIECE:(j + 1) * _W_OUT_PIECE],
                                 preferred_element_type=F32))
        pending.extend(functools.partial(piece, j) for j in range(d // _W_OUT_PIECE))

    def emit(k):
        for _ in range(min(k, len(pending))):
            pending.pop(0)()

    queue_in_proj(0)
    emit(len(pending))
    for s in range(n_sub):
        r0 = s * sub_rows
        if s + 1 < n_sub:
            queue_in_proj(s + 1)
        slots = cpsub + 2
        u = us[s]
        q = u["qk"][:, :D_K] * (DK ** -0.5)
        k = u["qk"][:, D_K:]
        v, r = u["v"], u["r"]

        z = jnp.dot(u["a_lr"].astype(BF16), wa2_ref[...], preferred_element_type=F32) + ba
        log_a = -(jnp.maximum(-z, 0.0) + jnp.log(1.0 + jnp.exp(-jnp.abs(z)))) * (1.0 / GATE_TAU)
        la1 = log_a.astype(BF16)
        rem = log_a - la1.astype(F32)
        la2 = rem.astype(BF16)
        la3 = (rem - la2.astype(F32)).astype(BF16)
        g_parts, glast_parts, decay = [], [], []
        for c in range(cpsub):
            sl = slice(c * chunk, (c + 1) * chunk)
            split = jnp.concatenate([la1[sl], la2[sl], la3[sl]], axis=0)
            g_c = jnp.dot(ltri, split, preferred_element_type=F32)
            gl = g_c[chunk - 1:chunk, :]
            g_parts.append(g_c)
            glast_parts.append(jnp.broadcast_to(gl, (chunk, D_K)))
            decay.append(jnp.exp(gl))
        g = jnp.concatenate(g_parts, axis=0)
        glast = jnp.concatenate(glast_parts, axis=0)
        sub_min = jnp.min(glast)
        g_min = sub_min if g_min is None else jnp.minimum(g_min, sub_min)

        qg = (q * jnp.exp(g)).astype(BF16)
        kgl = (k * jnp.exp(glast - g)).astype(BF16)
        if not direct:
            kng = (k * jnp.exp(-g)).astype(BF16)
        emit(-(-len(pending) // slots))
        slots -= 1

        gla_rows = []
        for c in range(cpsub):
            b = (r0 + c * chunk) // tt
            sl = slice(c * chunk, (c + 1) * chunk)
            s_t = state[b]
            v_c, r_c = v[sl], r[sl]
            x_st = jnp.concatenate([v_c[:, hh * DV:(hh + 1) * DV] for hh in range(N_HEADS)], axis=0)
            r_st = jnp.concatenate([r_c[:, hh * DV:(hh + 1) * DV] for hh in range(N_HEADS)], axis=0)
            if direct:
                att = _direct_scores(q[sl], k[sl], g[sl], row_ref, chunk)
            else:
                att = lax.dot_general(qg[sl], _head_stack(kng[sl], lane_head), (((1,), (1,)), ((), ())),
                                      preferred_element_type=F32)
            att = (att * cmask).astype(BF16)
            o_st = jnp.dot(_head_stack(att, score_head), x_st.astype(BF16), preferred_element_type=F32)
            o_st = o_st + lax.dot_general(_head_stack(qg[sl], lane_head), s_t.astype(BF16),
                                          (((1,), (1,)), ((), ())), preferred_element_type=F32)
            ds = jnp.dot(x_st.T.astype(BF16), _head_stack(kgl[sl], lane_head), preferred_element_type=F32)
            state[b] = s_t * decay[c] + ds
            ms = jnp.mean(o_st * o_st, axis=-1, keepdims=True)
            o_st = o_st * lax.rsqrt(ms + EPS) * g_head * _silu(r_st)
            gla_rows.append(jnp.concatenate([o_st[hh * chunk:(hh + 1) * chunk] for hh in range(N_HEADS)], axis=1))
            emit(-(-len(pending) // slots))
            slots -= 1
        o_gla = jnp.concatenate(gla_rows, axis=0)

        uc = u["gate_c"] * u["hconv"]
        seg = min(tt, sub_rows)
        row = lax.broadcasted_iota(jnp.int32, (seg, D_CONV), 0)
        conv_rows = []
        for i in range(sub_rows // seg):
            b = (r0 + i * seg) // tt
            ub = uc[i * seg:(i + 1) * seg]
            t1 = jnp.broadcast_to(tail[b][SUBLANES - 1:SUBLANES], (seg, D_CONV))
            t2 = jnp.broadcast_to(tail[b][SUBLANES - 2:SUBLANES - 1], (seg, D_CONV))
            u1 = jnp.where(row == 0, t1, pltpu.roll(ub, 1, 0))
            u2 = jnp.where(row == 0, t2, jnp.where(row == 1, t1, pltpu.roll(ub, 2, 0)))
            conv_rows.append(wc[0:1] * u2 + wc[1:2] * u1 + wc[2:3] * ub)
            tail[b] = ub[seg - SUBLANES:seg]
        o_conv = u["gate_b"] * jnp.concatenate(conv_rows, axis=0)
        emit(len(pending))
        queue_out_proj(s, jnp.concatenate([o_gla, o_conv], axis=1).astype(BF16))
        if s > 0:
            store_y(s - 1, xs[s - 1] + jnp.concatenate(ys[s - 1], axis=1))

    emit(len(pending))
    store_y(n_sub - 1, xs[n_sub - 1] + jnp.concatenate(ys[n_sub - 1], axis=1))

    for b in range(nb):
        stn_ref[b] = state[b]
        ctn_ref[b] = tail[b]
    return g_min


def _mix_body(x_ref, s0_ref, c0_ref, n_ref, win_ref, wa2_ref, ba_ref, gh_ref, wc_ref, wout_ref,
              ltri_ref, cmask_ref, o_ref, sout_ref, cout_ref, st_ref, ct_ref, stn_ref, ctn_ref, row_ref,
              *, nb, tt, chunk, sub_rows, bcast_state):
    t = pl.program_id(1)

    @pl.when(t == 0)
    def _load_state():
        for b in range(nb):
            sb = 0 if bcast_state else b
            st_ref[b] = s0_ref[sb].reshape(D_K, DV).T
            ct_ref[b] = c0_ref[sb]

    tile = functools.partial(
        _mix_tile, x_ref, n_ref, win_ref, wa2_ref, ba_ref, gh_ref, wc_ref, wout_ref, ltri_ref, cmask_ref,
        o_ref, st_ref, ct_ref, stn_ref, ctn_ref, row_ref, nb=nb, tt=tt, chunk=chunk, sub_rows=sub_rows)
    g_min = tile(direct=False)

    @pl.when(g_min < SAFE_LOG_DECAY)
    def _redo_without_factorisation():
        tile(direct=True)

    for b in range(nb):
        st_ref[b] = stn_ref[b]
        ct_ref[b] = ctn_ref[b]

    @pl.when(t == pl.num_programs(1) - 1)
    def _store_state():
        for b in range(nb):
            sout_ref[b] = stn_ref[b].T.reshape(N_HEADS, DK, DV)
            cout_ref[b] = ctn_ref[b]


def _mix(x, s0, c0, n, win_p, wa2_p, ba, gh, wc, wout, *, nb, tt):
    bsz, seq, d = x.shape
    nb = min(nb, bsz)
    tt = min(tt, seq)
    assert bsz % nb == 0 and seq % tt == 0
    chunk = min(GLA_CHUNK, tt)
    sub_rows = min(MIX_SUB_ROWS, nb * tt)
    assert tt % chunk == 0 and (nb * tt) % sub_rows == 0
    assert tt % sub_rows == 0 or sub_rows % tt == 0
    bcast_state = s0.shape[0] != bsz
    assert not bcast_state or (s0.shape[0] == 1 and c0.shape[0] == 1)
    nbs = 1 if bcast_state else nb
    state_idx = (lambda b, t: (0, 0, 0, 0)) if bcast_state else (lambda b, t: (b, 0, 0, 0))
    tail_idx = (lambda b, t: (0, 0, 0)) if bcast_state else (lambda b, t: (b, 0, 0))

    tri = np.tril(np.ones((chunk, chunk), np.float32))
    ltri = jnp.asarray(np.concatenate([tri, tri, tri], axis=1), dtype=BF16)
    cmask = jnp.asarray(np.tile(tri, (1, N_HEADS)), dtype=F32)

    body = functools.partial(_mix_body, nb=nb, tt=tt, chunk=chunk, sub_rows=sub_rows, bcast_state=bcast_state)
    return pl.pallas_call(
        body,
        grid=(bsz // nb, seq // tt),
        in_specs=[
            pl.BlockSpec((nb, tt, d), lambda b, t: (b, t, 0)),
            pl.BlockSpec((nbs, N_HEADS, DK, DV), state_idx),
            pl.BlockSpec((nbs, SUBLANES, D_CONV), tail_idx),
            _const_spec(n.shape),
            _const_spec(win_p.shape),
            _const_spec(wa2_p.shape),
            _const_spec(ba.shape),
            _const_spec(gh.shape),
            _const_spec(wc.shape),
            _const_spec(wout.shape),
            _const_spec(ltri.shape),
            _const_spec(cmask.shape),
        ],
        out_specs=[
            pl.BlockSpec((nb, tt, d), lambda b, t: (b, t, 0)),
            pl.BlockSpec((nb, N_HEADS, DK, DV), lambda b, t: (b, 0, 0, 0)),
            pl.BlockSpec((nb, SUBLANES, D_CONV), lambda b, t: (b, 0, 0)),
        ],
        out_shape=[
            jax.ShapeDtypeStruct((bsz, seq, d), x.dtype),
            jax.ShapeDtypeStruct((bsz, N_HEADS, DK, DV), F32),
            jax.ShapeDtypeStruct((bsz, SUBLANES, D_CONV), F32),
        ],
        scratch_shapes=[
            pltpu.VMEM((nb, DV, D_K), F32),
            pltpu.VMEM((nb, SUBLANES, D_CONV), F32),
            pltpu.VMEM((nb, DV, D_K), F32),
            pltpu.VMEM((nb, SUBLANES, D_CONV), F32),
            pltpu.VMEM((2, chunk, D_K), F32),
        ],
        compiler_params=pltpu.CompilerParams(
            dimension_semantics=("parallel", "arbitrary"), vmem_limit_bytes=VMEM_LIMIT_BYTES),
        name="mix",
    )(x, s0, c0, n, win_p, wa2_p, ba, gh, wc, wout, ltri, cmask)


def _pack_mix(w_in, w_a2):
    d = w_in.shape[0]
    lo = 2 * D_K + 2 * D_GLA
    hi = lo + GATE_RANK
    pad = jnp.zeros((d, LANES - GATE_RANK), w_in.dtype)
    win_p = jnp.concatenate([w_in[:, :lo], w_in[:, hi:], w_in[:, lo:hi], pad], axis=1).astype(BF16)
    wa2_p = jnp.concatenate([w_a2, jnp.zeros((LANES - GATE_RANK, w_a2.shape[1]), w_a2.dtype)], axis=0).astype(BF16)
    return win_p, wa2_p


def kernel(x_prompt, x_sample, state_gla, cache_conv, meta, norm_ffn1, w_up1, w_down1, norm_mix, w_in, w_a2, b_a, g_head, w_conv, w_out, norm_ffn2, w_up2, w_down2, norm_final):
    bp, seq, d = x_prompt.shape
    bs, seq_s, _ = x_sample.shape
    depth = w_in.shape[0]
    dt = x_prompt.dtype

    xm = meta.astype(dt)[None]
    xp, xs = x_prompt, x_sample
    sm = jnp.zeros((1, N_HEADS, DK, DV), F32)
    cm = jnp.zeros((1, SUBLANES, D_CONV), dt)
    nf = norm_final.reshape(1, d)
    sp_l, cp_l, ss_l, cs_l = [], [], [], []
    for i in range(depth):
        last = i == depth - 1
        wup1_p, wdn1_p = w_up1[i].astype(BF16), w_down1[i].astype(BF16)
        wup2_p, wdn2_p = w_up2[i].astype(BF16), w_down2[i].astype(BF16)
        win_p, wa2_p = _pack_mix(w_in[i], w_a2[i])
        mixw = (norm_mix[i].reshape(1, d), win_p, wa2_p, b_a[i].reshape(1, D_K), g_head[i].reshape(1, DV),
                w_conv[i], w_out[i].astype(BF16))
        n1 = norm_ffn1[i].reshape(1, d)
        n3 = norm_ffn2[i].reshape(1, d)
        cs0 = jnp.pad(cache_conv[i].astype(dt), ((0, 0), (SUBLANES - (CONV_W - 1), 0), (0, 0)))

        def ffn1(x):
            return _ffn(x.reshape(-1, d), n1, wup1_p, wdn1_p, nf, final_norm=False, rows=512).reshape(x.shape)

        def ffn2(x):
            return _ffn(x.reshape(-1, d), n3, wup2_p, wdn2_p, nf, final_norm=last, rows=512).reshape(x.shape)

        xm, sm_new, cm_new = _mix(ffn1(xm), sm, cm, *mixw, nb=1, tt=meta.shape[0])
        xp, sp, cp = _mix(ffn1(xp), sm_new, cm_new, *mixw, nb=1, tt=512)
        xs, ss, cs = _mix(ffn1(xs), state_gla[i].astype(F32), cs0, *mixw, nb=8, tt=seq_s)
        xp = ffn2(xp)
        xs = ffn2(xs)
        if not last:
            xm = ffn2(xm)
        sp_l.append(sp.astype(state_gla.dtype))
        cp_l.append(cp[:, SUBLANES - (CONV_W - 1):].astype(cache_conv.dtype))
        ss_l.append(ss.astype(state_gla.dtype))
        cs_l.append(cs[:, SUBLANES - (CONV_W - 1):].astype(cache_conv.dtype))
    return (xp, xs, jnp.stack(sp_l), jnp.stack(cp_l), jnp.stack(ss_l), jnp.stack(cs_l))
```

```python
import functools

import numpy as np
import jax
import jax.numpy as jnp
from jax import lax
from jax.experimental import pallas as pl
from jax.experimental.pallas import tpu as pltpu

EPS = 1e-6
N_HEADS = 4
DK = 64
DV = 128
D_K = N_HEADS * DK
D_GLA = N_HEADS * DV
D_CONV = 512
GATE_RANK = 16
GATE_TAU = 16.0
CONV_W = 3
GLA_CHUNK = 64
FF_CHUNK = 256
MIX_SUB_ROWS = 256
SAFE_LOG_DECAY = -60.0
LANES = 128
SUBLANES = 8
VMEM_LIMIT_BYTES = 56 * 1024 * 1024

F32 = jnp.float32
BF16 = jnp.bfloat16


def _rms(x, g):
    ms = jnp.mean(x * x, axis=-1, keepdims=True)
    return x * lax.rsqrt(ms + EPS) * g


def _silu(x):
    return x * (1.0 / (1.0 + jnp.exp(-x)))


def _const_spec(shape):
    nd = len(shape)
    return pl.BlockSpec(shape, lambda *_: (0,) * nd, pipeline_mode=pl.Buffered(1))


def _ffn_items(x, n_ref, wup_ref, wdn_ref, out):
    h = _rms(x, n_ref[...]).astype(BF16)
    f = wdn_ref.shape[0]
    yield
    acc = None
    for c0 in range(0, f, FF_CHUNK):
        gate = jnp.dot(h, wup_ref[:, c0:c0 + FF_CHUNK], preferred_element_type=F32)
        yield
        up = jnp.dot(h, wup_ref[:, f + c0:f + c0 + FF_CHUNK], preferred_element_type=F32)
        yield
        a = (_silu(gate) * up).astype(BF16)
        yield
        part = jnp.dot(a, wdn_ref[c0:c0 + FF_CHUNK, :], preferred_element_type=F32)
        acc = part if acc is None else acc + part
        yield
    out.append(x + 0.5 * acc)


def _ffn_body(x_ref, n_ref, wup_ref, wdn_ref, nf_ref, o_ref, *, final_norm):
    out = []
    for _ in _ffn_items(x_ref[...], n_ref, wup_ref, wdn_ref, out):
        pass
    y = out[0]
    if final_norm:
        y = _rms(y, nf_ref[...])
    o_ref[...] = y


def _ffn(x2d, n, wup_p, wdn_p, nf, *, final_norm, rows):
    m, d = x2d.shape
    rows = min(rows, m)
    assert m % rows == 0 and wdn_p.shape[0] % FF_CHUNK == 0
    body = functools.partial(_ffn_body, final_norm=final_norm)
    return pl.pallas_call(
        body,
        grid=(m // rows,),
        in_specs=[
            pl.BlockSpec((rows, d), lambda i: (i, 0)),
            _const_spec(n.shape),
            _const_spec(wup_p.shape),
            _const_spec(wdn_p.shape),
            _const_spec(nf.shape),
        ],
        out_specs=pl.BlockSpec((rows, d), lambda i: (i, 0)),
        out_shape=jax.ShapeDtypeStruct((m, d), x2d.dtype),
        compiler_params=pltpu.CompilerParams(
            dimension_semantics=("parallel",), vmem_limit_bytes=VMEM_LIMIT_BYTES),
        name="ffn_final" if final_norm else "ffn",
    )(x2d, n, wup_p, wdn_p, nf)


def _head_stack(a, head_of_lane):
    zero = jnp.zeros((), a.dtype)
    return jnp.concatenate([jnp.where(head_of_lane == hh, a, zero) for hh in range(N_HEADS)], axis=0)


def _direct_scores(q_c, k_c, g_c, row_ref, chunk):
    hc = N_HEADS * chunk
    row_ref[0] = k_c
    row_ref[1] = g_c
    lane_head = lax.broadcasted_iota(jnp.int32, (1, D_K), 1) // DK
    score_head = lax.broadcasted_iota(jnp.int32, (1, hc), 1) // chunk
    key_of_lane = lax.broadcasted_iota(jnp.int32, (chunk, hc), 1) % chunk

    def one_key(j, att):
        kj = row_ref[0, pl.ds(j, 1), :]
        gj = row_ref[1, pl.ds(j, 1), :]
        p = q_c * kj * jnp.exp(jnp.minimum(g_c - gj, 0.0))
        col = jnp.zeros((chunk, hc), F32)
        for hh in range(N_HEADS):
            s_h = jnp.sum(jnp.where(lane_head == hh, p, 0.0), axis=-1, keepdims=True)
            col = jnp.where(score_head == hh, s_h, col)
        return jnp.where(key_of_lane == j, col, att)

    return lax.fori_loop(0, chunk, one_key, jnp.zeros((chunk, hc), F32))


_Q0, _K0, _V0, _R0, _GB0, _GC0, _HC0, _A0 = 0, 256, 512, 1024, 1536, 2048, 2560, 3072
_U_WIDTH = _A0 + LANES
_U_PIECE = 256
_U_ORDER = (_A0, _Q0, _K0, _GC0, _HC0, _GB0, _GC0 + 256, _HC0 + 256, _GB0 + 256, _V0, _V0 + 256, _R0, _R0 + 256)
_W_OUT_PIECE = 512


def _mix_tile(x_ref, n_ref, win_ref, wa2_ref, ba_ref, gh_ref, wc_ref, wout_ref, ltri_ref, cmask_ref,
              o_ref, st_ref, ct_ref, stn_ref, ctn_ref, row_ref, *, nb, tt, chunk, sub_rows, direct, side=None):
    rows = nb * tt
    n_sub = rows // sub_rows
    cpsub = sub_rows // chunk
    hc = N_HEADS * chunk
    d = x_ref.shape[-1]

    lane_head = lax.broadcasted_iota(jnp.int32, (1, D_K), 1) // DK
    score_head = lax.broadcasted_iota(jnp.int32, (1, hc), 1) // chunk
    ltri = ltri_ref[...]
    cmask = cmask_ref[...]
    g_head = gh_ref[...]
    wc = wc_ref[...]
    norm_g = n_ref[...]
    ba = ba_ref[...]

    state = {b: st_ref[b] for b in range(nb)}
    tail = {b: ct_ref[b] for b in range(nb)}
    g_mins = []

    def load_x(s):
        return x_ref[s * sub_rows:(s + 1) * sub_rows, :]

    def store_y(s, y):
        r0 = s * sub_rows
        if tt >= sub_rows:
            b0, t0 = divmod(r0, tt)
            o_ref[b0, t0:t0 + sub_rows, :] = y
        else:
            o_ref[r0 // tt:(r0 + sub_rows) // tt] = y.reshape(sub_rows // tt, tt, d)

    xs, us, ys = {}, {}, {}
    pending = []

    def piece_width(c0):
        return min(_U_PIECE, _U_WIDTH - c0)

    def queue_in_proj(s):
        xs[s] = load_x(s)
        h = _rms(xs[s], norm_g).astype(BF16)
        us[s] = {}

        def piece(c0):
            us[s][c0] = jnp.dot(h, win_ref[:, c0:c0 + piece_width(c0)], preferred_element_type=F32)
        pending.extend(functools.partial(piece, c0) for c0 in _U_ORDER)

    def queue_out_proj(s, part, lhs, k0):
        def piece(j):
            ys[s][part].append(jnp.dot(lhs, wout_ref[k0:k0 + lhs.shape[1], j * _W_OUT_PIECE:(j + 1) * _W_OUT_PIECE],
                                       preferred_element_type=F32))
        ys[s][part] = []
        pending.extend(functools.partial(piece, j) for j in range(wout_ref.shape[1] // _W_OUT_PIECE))

    def emit(k):
        for _ in range(min(k, len(pending))):
            pending.pop(0)()

    def side_step():
        if side is not None:
            next(side, None)

    def finish(s):
        y = xs[s] + jnp.concatenate(ys[s]["conv"], axis=1) + jnp.concatenate(ys[s]["gla"], axis=1)
        store_y(s, y)

    d_out = wout_ref.shape[1]
    queue_in_proj(0)
    for _ in range(3):
        emit(1)
        side_step()

    def stage(s):
        r0 = s * sub_rows

        def ucol(c0, w):
            p0 = c0 - c0 % _U_PIECE
            while p0 not in us[s]:
                emit(1)
            return us[s][p0][:, c0 - p0:c0 - p0 + w]

        z = jnp.dot(ucol(_A0, LANES).astype(BF16), wa2_ref[...], preferred_element_type=F32) + ba
        log_a = -(jnp.maximum(-z, 0.0) + jnp.log(1.0 + jnp.exp(-jnp.abs(z)))) * (1.0 / GATE_TAU)
        yield
        la1 = log_a.astype(BF16)
        rem = log_a - la1.astype(F32)
        la2 = rem.astype(BF16)
        la3 = (rem - la2.astype(F32)).astype(BF16)
        yield
        g_parts, glast_parts, decay = [], [], []
        for c in range(cpsub):
            sl = slice(c * chunk, (c + 1) * chunk)
            split = jnp.concatenate([la1[sl], la2[sl], la3[sl]], axis=0)
            g_c = jnp.dot(ltri, split, preferred_element_type=F32)
            gl = g_c[chunk - 1:chunk, :]
            g_parts.append(g_c)
            glast_parts.append(jnp.broadcast_to(gl, (chunk, D_K)))
            decay.append(jnp.exp(gl))
        g = jnp.concatenate(g_parts, axis=0)
        glast = jnp.concatenate(glast_parts, axis=0)
        g_mins.append(jnp.min(glast))
        yield
        q = ucol(_Q0, D_K) * (DK ** -0.5)
        k = ucol(_K0, D_K)
        qg = (q * jnp.exp(g)).astype(BF16)
        yield
        kgl = (k * jnp.exp(glast - g)).astype(BF16)
        yield
        if not direct:
            kng = (k * jnp.exp(-g)).astype(BF16)
            yield

        seg = min(tt, sub_rows)
        row = lax.broadcasted_iota(jnp.int32, (seg, _U_PIECE), 0)
        conv_halves, new_tail = [], {}
        for half in range(D_CONV // _U_PIECE):
            c_off = half * _U_PIECE
            uc = ucol(_GC0 + c_off, _U_PIECE) * ucol(_HC0 + c_off, _U_PIECE)
            conv_rows = []
            for i in range(sub_rows // seg):
                b = (r0 + i * seg) // tt
                ub = uc[i * seg:(i + 1) * seg]
                tl = tail[b][:, c_off:c_off + _U_PIECE]
                t1 = jnp.broadcast_to(tl[SUBLANES - 1:SUBLANES], (seg, _U_PIECE))
                t2 = jnp.broadcast_to(tl[SUBLANES - 2:SUBLANES - 1], (seg, _U_PIECE))
                u1 = jnp.where(row == 0, t1, pltpu.roll(ub, 1, 0))
                u2 = jnp.where(row == 0, t2, jnp.where(row == 1, t1, pltpu.roll(ub, 2, 0)))
                wch = wc[:, c_off:c_off + _U_PIECE]
                conv_rows.append(wch[0:1] * u2 + wch[1:2] * u1 + wch[2:3] * ub)
                new_tail.setdefault(b, []).append(ub[seg - SUBLANES:seg])
            conv_halves.append(ucol(_GB0 + c_off, _U_PIECE) * jnp.concatenate(conv_rows, axis=0))
            yield
        for b, parts in new_tail.items():
            tail[b] = jnp.concatenate(parts, axis=1)
        queue_out_proj(s, "conv", jnp.concatenate(conv_halves, axis=1).astype(BF16), D_GLA)

        sls = [slice(c * chunk, (c + 1) * chunk) for c in range(cpsub)]
        x_st, att, ds = [], [], []
        for c, sl in enumerate(sls):
            x_st.append(jnp.concatenate([ucol(_V0 + hh * DV, DV)[sl] for hh in range(N_HEADS)], axis=0))
            if direct:
                a_c = _direct_scores(q[sl], k[sl], g[sl], row_ref, chunk)
            else:
                a_c = lax.dot_general(qg[sl], _head_stack(kng[sl], lane_head), (((1,), (1,)), ((), ())),
                                      preferred_element_type=F32)
            att.append(a_c)
            yield
            ds.append(jnp.dot(x_st[c].T.astype(BF16), _head_stack(kgl[sl], lane_head), preferred_element_type=F32))
            yield
        o_st = []
        for c, sl in enumerate(sls):
            b = (r0 + c * chunk) // tt
            a_c = (att[c] * cmask).astype(BF16)
            o_c = jnp.dot(_head_stack(a_c, score_head), x_st[c].astype(BF16), preferred_element_type=F32)
            o_c = o_c + lax.dot_general(_head_stack(qg[sl], lane_head), state[b].astype(BF16),
                                        (((1,), (1,)), ((), ())), preferred_element_type=F32)
            o_st.append(o_c)
            state[b] = state[b] * decay[c] + ds[c]
            yield
        gla_rows = []
        for c, sl in enumerate(sls):
            r_st = jnp.concatenate([ucol(_R0 + hh * DV, DV)[sl] for hh in range(N_HEADS)], axis=0)
            ms = jnp.mean(o_st[c] * o_st[c], axis=-1, keepdims=True)
            o_c = o_st[c] * lax.rsqrt(ms + EPS) * g_head * _silu(r_st)
            gla_rows.append(jnp.concatenate([o_c[hh * chunk:(hh + 1) * chunk] for hh in range(N_HEADS)], axis=1))
            yield
        queue_out_proj(s, "gla", jnp.concatenate(gla_rows, axis=0).astype(BF16), 0)

    for s in range(n_sub):
        ys[s] = {}
        if s + 1 < n_sub:
            queue_in_proj(s + 1)
        for _ in stage(s):
            emit(1)
            side_step()
        if s > 0:
            finish(s - 1)
        emit(len(pending) - (d_out // _W_OUT_PIECE))

    emit(len(pending))
    finish(n_sub - 1)
    if side is not None:
        for _ in side:
            pass

    for b in range(nb):
        stn_ref[b] = state[b]
        ctn_ref[b] = tail[b]
    return functools.reduce(jnp.minimum, g_mins)


def _ffn_mix_body(x0_ref, xn_ref, s0_ref, c0_ref, n1_ref, wup_ref, wdn_ref, n2_ref, win_ref, wa2_ref, ba_ref,
                  gh_ref, wc_ref, wout_ref, ltri_ref, cmask_ref, o_ref, sout_ref, cout_ref,
                  st_ref, ct_ref, stn_ref, ctn_ref, row_ref, x1_ref, *, nb, tt, tps, chunk, sub_rows, bcast_state):
    i = pl.program_id(0)
    t = lax.rem(i, tps)
    slot = lax.rem(i, 2)
    rows = nb * tt
    d = x0_ref.shape[-1]

    @pl.when(t == 0)
    def _load_state():
        for b in range(nb):
            sb = 0 if bcast_state else b
            st_ref[b] = s0_ref[sb].reshape(D_K, DV).T
            ct_ref[b] = c0_ref[sb]

    @pl.when(i == 0)
    def _first_ffn():
        out = []
        for _ in _ffn_items(x0_ref[...].reshape(rows, d), n1_ref, wup_ref, wdn_ref, out):
            pass
        x1_ref[0] = out[0]

    tile = functools.partial(
        _mix_tile, x1_ref.at[slot], n2_ref, win_ref, wa2_ref, ba_ref, gh_ref, wc_ref, wout_ref, ltri_ref,
        cmask_ref, o_ref, st_ref, ct_ref, stn_ref, ctn_ref, row_ref, nb=nb, tt=tt, chunk=chunk, sub_rows=sub_rows)
    next_x1 = []
    g_min = tile(direct=False,
                 side=_ffn_items(xn_ref[...].reshape(rows, d), n1_ref, wup_ref, wdn_ref, next_x1))
    x1_ref[1 - slot] = next_x1[0]

    @pl.when(g_min < SAFE_LOG_DECAY)
    def _redo_without_factorisation():
        tile(direct=True)

    for b in range(nb):
        st_ref[b] = stn_ref[b]
        ct_ref[b] = ctn_ref[b]

    @pl.when(t == tps - 1)
    def _store_state():
        for b in range(nb):
            sout_ref[b] = stn_ref[b].T.reshape(N_HEADS, DK, DV)
            cout_ref[b] = ctn_ref[b]


def _ffn_mix(x, s0, c0, n1, wup_p, wdn_p, n2, win_p, wa2_p, ba, gh, wc, wout, *, nb, tt):
    bsz, seq, d = x.shape
    nb = min(nb, bsz)
    tt = min(tt, seq)
    assert bsz % nb == 0 and seq % tt == 0
    chunk = min(GLA_CHUNK, tt)
    sub_rows = min(MIX_SUB_ROWS, nb * tt)
    assert tt % chunk == 0 and (nb * tt) % sub_rows == 0
    assert (nb == 1 and tt % sub_rows == 0) or sub_rows % tt == 0
    bcast_state = s0.shape[0] != bsz
    assert not bcast_state or (s0.shape[0] == 1 and c0.shape[0] == 1)
    nbs = 1 if bcast_state else nb
    tps = seq // tt
    n_tiles = (bsz // nb) * tps
    state_idx = (lambda i: (0, 0, 0, 0)) if bcast_state else (lambda i: (i // tps, 0, 0, 0))
    tail_idx = (lambda i: (0, 0, 0)) if bcast_state else (lambda i: (i // tps, 0, 0))

    def nxt(i):
        return jnp.minimum(i + 1, n_tiles - 1)

    tri = np.tril(np.ones((chunk, chunk), np.float32))
    ltri = jnp.asarray(np.concatenate([tri, tri, tri], axis=1), dtype=BF16)
    cmask = jnp.asarray(np.tile(tri, (1, N_HEADS)), dtype=F32)

    body = functools.partial(_ffn_mix_body, nb=nb, tt=tt, tps=tps, chunk=chunk, sub_rows=sub_rows,
                             bcast_state=bcast_state)
    consts = (n1, wup_p, wdn_p, n2, win_p, wa2_p, ba, gh, wc, wout, ltri, cmask)
    return pl.pallas_call(
        body,
        grid=(n_tiles,),
        in_specs=[
            pl.BlockSpec((nb, tt, d), lambda i: (0, 0, 0), pipeline_mode=pl.Buffered(1)),
            pl.BlockSpec((nb, tt, d), lambda i: (nxt(i) // tps, nxt(i) % tps, 0)),
            pl.BlockSpec((nbs, N_HEADS, DK, DV), state_idx),
            pl.BlockSpec((nbs, SUBLANES, D_CONV), tail_idx),
        ] + [_const_spec(c.shape) for c in consts],
        out_specs=[
            pl.BlockSpec((nb, tt, d), lambda i: (i // tps, i % tps, 0)),
            pl.BlockSpec((nb, N_HEADS, DK, DV), lambda i: (i // tps, 0, 0, 0)),
            pl.BlockSpec((nb, SUBLANES, D_CONV), lambda i: (i // tps, 0, 0)),
        ],
        out_shape=[
            jax.ShapeDtypeStruct((bsz, seq, d), x.dtype),
            jax.ShapeDtypeStruct((bsz, N_HEADS, DK, DV), F32),
            jax.ShapeDtypeStruct((bsz, SUBLANES, D_CONV), F32),
        ],
        scratch_shapes=[
            pltpu.VMEM((nb, DV, D_K), F32),
            pltpu.VMEM((nb, SUBLANES, D_CONV), F32),
            pltpu.VMEM((nb, DV, D_K), F32),
            pltpu.VMEM((nb, SUBLANES, D_CONV), F32),
            pltpu.VMEM((2, chunk, D_K), F32),
            pltpu.VMEM((2, nb * tt, d), F32),
        ],
        compiler_params=pltpu.CompilerParams(
            dimension_semantics=("arbitrary",), vmem_limit_bytes=VMEM_LIMIT_BYTES),
        name="ffn_mix",
    )(x, x, s0, c0, *consts)


def _pack_mix(w_in, w_a2):
    d = w_in.shape[0]
    lo = 2 * D_K + 2 * D_GLA
    hi = lo + GATE_RANK
    pad = jnp.zeros((d, LANES - GATE_RANK), w_in.dtype)
    win_p = jnp.concatenate([w_in[:, :lo], w_in[:, hi:], w_in[:, lo:hi], pad], axis=1).astype(BF16)
    wa2_p = jnp.concatenate([w_a2, jnp.zeros((LANES - GATE_RANK, w_a2.shape[1]), w_a2.dtype)], axis=0).astype(BF16)
    return win_p, wa2_p


def kernel(x_prompt, x_sample, state_gla, cache_conv, meta, norm_ffn1, w_up1, w_down1, norm_mix, w_in, w_a2, b_a, g_head, w_conv, w_out, norm_ffn2, w_up2, w_down2, norm_final):
    bp, seq, d = x_prompt.shape
    bs, seq_s, _ = x_sample.shape
    depth = w_in.shape[0]
    dt = x_prompt.dtype

    xm = meta.astype(dt)[None]
    xp, xs = x_prompt, x_sample
    sm = jnp.zeros((1, N_HEADS, DK, DV), F32)
    cm = jnp.zeros((1, SUBLANES, D_CONV), dt)
    nf = norm_final.reshape(1, d)
    sp_l, cp_l, ss_l, cs_l = [], [], [], []
    for i in range(depth):
        last = i == depth - 1
        wup1_p, wdn1_p = w_up1[i].astype(BF16), w_down1[i].astype(BF16)
        wup2_p, wdn2_p = w_up2[i].astype(BF16), w_down2[i].astype(BF16)
        win_p, wa2_p = _pack_mix(w_in[i], w_a2[i])
        lw = (norm_ffn1[i].reshape(1, d), wup1_p, wdn1_p,
              norm_mix[i].reshape(1, d), win_p, wa2_p, b_a[i].reshape(1, D_K), g_head[i].reshape(1, DV),
              w_conv[i], w_out[i].astype(BF16))
        n3 = norm_ffn2[i].reshape(1, d)
        cs0 = jnp.pad(cache_conv[i].astype(dt), ((0, 0), (SUBLANES - (CONV_W - 1), 0), (0, 0)))

        def ffn2(x):
            return _ffn(x.reshape(-1, d), n3, wup2_p, wdn2_p, nf, final_norm=last, rows=512).reshape(x.shape)

        xm, sm_new, cm_new = _ffn_mix(xm, sm, cm, *lw, nb=1, tt=meta.shape[0])
        xp, sp, cp = _ffn_mix(xp, sm_new, cm_new, *lw, nb=1, tt=512)
        xs, ss, cs = _ffn_mix(xs, state_gla[i].astype(F32), cs0, *lw, nb=4, tt=seq_s)
        xp = ffn2(xp)
        xs = ffn2(xs)
        if not last:
            xm = ffn2(xm)
        sp_l.append(sp.astype(state_gla.dtype))
        cp_l.append(cp[:, SUBLANES - (CONV_W - 1):].astype(cache_conv.dtype))
        ss_l.append(ss.astype(state_gla.dtype))
        cs_l.append(cs[:, SUBLANES - (CONV_W - 1):].astype(cache_conv.dtype))
    return (xp, xs, jnp.stack(sp_l), jnp.stack(cp_l), jnp.stack(ss_l), jnp.stack(cs_l))
```

```python
import functools

import numpy as np
import jax
import jax.numpy as jnp
from jax import lax
from jax.experimental import pallas as pl
from jax.experimental.pallas import tpu as pltpu

EPS = 1e-6
N_HEADS = 4
DK = 64
DV = 128
D_K = N_HEADS * DK
D_GLA = N_HEADS * DV
D_CONV = 512
GATE_RANK = 16
GATE_TAU = 16.0
CONV_W = 3
GLA_CHUNK = 64
FF_CHUNK = 512
MIX_SUB_ROWS = 256
TILE_ROWS = 512
SAMPLE_TILE_SEQS = 4
SAFE_LOG_DECAY = -60.0
LANES = 128
SUBLANES = 8
VMEM_LIMIT_BYTES = 56 * 1024 * 1024

F32 = jnp.float32
BF16 = jnp.bfloat16


def _rms(x, g):
    ms = jnp.mean(x * x, axis=-1, keepdims=True)
    return x * lax.rsqrt(ms + EPS) * g


def _silu(x):
    return x * (1.0 / (1.0 + jnp.exp(-x)))


def _const_spec(shape):
    nd = len(shape)
    return pl.BlockSpec(shape, lambda *_: (0,) * nd, pipeline_mode=pl.Buffered(1))


def _ffn_items(x, n_ref, wup_ref, wdn_ref, out):
    h = _rms(x, n_ref[...]).astype(BF16)
    f = wdn_ref.shape[0]
    yield
    acc = None
    bounds = list(range(0, f, FF_CHUNK)) + [f]
    for c0, c1 in zip(bounds[:-1], bounds[1:]):
        gate = jnp.dot(h, wup_ref[:, c0:c1], preferred_element_type=F32)
        yield
        up = jnp.dot(h, wup_ref[:, f + c0:f + c1], preferred_element_type=F32)
        yield
        a = (_silu(gate) * up).astype(BF16)
        yield
        part = jnp.dot(a, wdn_ref[c0:c1, :], preferred_element_type=F32)
        acc = part if acc is None else acc + part
        yield
    out.append(x + 0.5 * acc)


def _ffn_body(x_ref, n_ref, wup_ref, wdn_ref, nf_ref, o_ref, *, final_norm):
    rows = x_ref.shape[0]
    n_streams = 2 if rows == 2 * TILE_ROWS else 1
    block = rows // n_streams
    outs = [[] for _ in range(n_streams)]
    gens = [_ffn_items(x_ref[k * block:(k + 1) * block, :], n_ref, wup_ref, wdn_ref, outs[k])
            for k in range(n_streams)]

    def finish(k):
        y = outs[k][0]
        if final_norm:
            y = _rms(y, nf_ref[...])
        o_ref[k * block:(k + 1) * block, :] = y

    lag = 2 * -(-wdn_ref.shape[0] // FF_CHUNK)
    done = [False] * n_streams
    step = 0
    while not all(done):
        for k in range(n_streams):
            if done[k] or step < k * lag:
                continue
            try:
                next(gens[k])
            except StopIteration:
                done[k] = True
                finish(k)
        step += 1


def _ffn(x2d, n, wup_p, wdn_p, nf, *, final_norm, rows):
    m, d = x2d.shape
    rows = min(rows, m)
    assert m % rows == 0 and wdn_p.shape[0] % LANES == 0
    body = functools.partial(_ffn_body, final_norm=final_norm)
    return pl.pallas_call(
        body,
        grid=(m // rows,),
        in_specs=[
            pl.BlockSpec((rows, d), lambda i: (i, 0)),
            _const_spec(n.shape),
            _const_spec(wup_p.shape),
            _const_spec(wdn_p.shape),
            _const_spec(nf.shape),
        ],
        out_specs=pl.BlockSpec((rows, d), lambda i: (i, 0)),
        out_shape=jax.ShapeDtypeStruct((m, d), x2d.dtype),
        compiler_params=pltpu.CompilerParams(
            dimension_semantics=("parallel",), vmem_limit_bytes=VMEM_LIMIT_BYTES),
        name="ffn_final" if final_norm else "ffn",
    )(x2d, n, wup_p, wdn_p, nf)


def _head_stack(a, head_of_lane):
    zero = jnp.zeros((), a.dtype)
    return jnp.concatenate([jnp.where(head_of_lane == hh, a, zero) for hh in range(N_HEADS)], axis=0)


def _direct_scores(q_c, k_c, g_c, row_ref, chunk):
    hc = N_HEADS * chunk
    row_ref[0] = k_c
    row_ref[1] = g_c
    lane_head = lax.broadcasted_iota(jnp.int32, (1, D_K), 1) // DK
    score_head = lax.broadcasted_iota(jnp.int32, (1, hc), 1) // chunk
    key_of_lane = lax.broadcasted_iota(jnp.int32, (chunk, hc), 1) % chunk

    def one_key(j, att):
        kj = row_ref[0, pl.ds(j, 1), :]
        gj = row_ref[1, pl.ds(j, 1), :]
        p = q_c * kj * jnp.exp(jnp.minimum(g_c - gj, 0.0))
        col = jnp.zeros((chunk, hc), F32)
        for hh in range(N_HEADS):
            s_h = jnp.sum(jnp.where(lane_head == hh, p, 0.0), axis=-1, keepdims=True)
            col = jnp.where(score_head == hh, s_h, col)
        return jnp.where(key_of_lane == j, col, att)

    return lax.fori_loop(0, chunk, one_key, jnp.zeros((chunk, hc), F32))


_Q0, _K0, _V0 = 0, D_K, 2 * D_K
_R0 = _V0 + D_GLA
_GB0 = _R0 + D_GLA
_GC0 = _GB0 + D_CONV
_HC0 = _GC0 + D_CONV
_A0 = _HC0 + D_CONV
_U_WIDTH = _A0 + LANES
_U_PIECE = 256
_U_ORDER = (_A0, _Q0, _K0, _GC0, _HC0, _GB0, _GC0 + _U_PIECE, _HC0 + _U_PIECE, _GB0 + _U_PIECE,
            _V0, _V0 + _U_PIECE, _R0, _R0 + _U_PIECE)
_W_OUT_PIECE = 512


def _mix_tile(x_ref, n_ref, win_ref, wa2_ref, ba_ref, gh_ref, wc_ref, wout_ref, ltri_ref, cmask_ref,
              o_ref, st_ref, ct_ref, stn_ref, ctn_ref, row_ref, *, nb, tt, chunk, sub_rows, direct, side=None):
    rows = nb * tt
    n_sub = rows // sub_rows
    cpsub = sub_rows // chunk
    hc = N_HEADS * chunk
    d = x_ref.shape[-1]

    lane_head = lax.broadcasted_iota(jnp.int32, (1, D_K), 1) // DK
    score_head = lax.broadcasted_iota(jnp.int32, (1, hc), 1) // chunk
    ltri = ltri_ref[...]
    cmask = cmask_ref[...]
    g_head = gh_ref[...]
    wc = wc_ref[...]
    norm_g = n_ref[...]
    ba = ba_ref[...]

    state = {b: st_ref[b] for b in range(nb)}
    tail = {b: ct_ref[b] for b in range(nb)}
    g_mins = []

    def load_x(s):
        return x_ref[s * sub_rows:(s + 1) * sub_rows, :]

    def store_y(s, y):
        r0 = s * sub_rows
        if tt >= sub_rows:
            b0, t0 = divmod(r0, tt)
            o_ref[b0, t0:t0 + sub_rows, :] = y
        else:
            o_ref[r0 // tt:(r0 + sub_rows) // tt] = y.reshape(sub_rows // tt, tt, d)

    xs, us, ys = {}, {}, {}
    pending = []

    def piece_width(c0):
        return min(_U_PIECE, _U_WIDTH - c0)

    def queue_in_proj(s):
        xs[s] = load_x(s)
        h = _rms(xs[s], norm_g).astype(BF16)
        us[s] = {}

        def piece(c0):
            ref, off = (win_ref[0], 0) if c0 < _GB0 else (win_ref[1], _GB0) if c0 < _A0 else (win_ref[2], _A0)
            us[s][c0] = jnp.dot(h, ref[:, c0 - off:c0 - off + piece_width(c0)], preferred_element_type=F32)
        pending.extend(functools.partial(piece, c0) for c0 in _U_ORDER)

    def queue_out_proj(s, part, lhs, k0):
        def piece(j):
            ys[s][part].append(jnp.dot(lhs, wout_ref[k0:k0 + lhs.shape[1], j * _W_OUT_PIECE:(j + 1) * _W_OUT_PIECE],
                                       preferred_element_type=F32))
        ys[s][part] = []
        pending.extend(functools.partial(piece, j) for j in range(wout_ref.shape[1] // _W_OUT_PIECE))

    def emit(k):
        for _ in range(min(k, len(pending))):
            pending.pop(0)()

    def side_step():
        if side is not None:
            next(side, None)

    def finish(s):
        y = xs[s] + jnp.concatenate(ys[s]["conv"], axis=1) + jnp.concatenate(ys[s]["gla"], axis=1)
        store_y(s, y)

    d_out = wout_ref.shape[1]
    queue_in_proj(0)
    for _ in range(3):
        emit(1)
        side_step()

    def stage(s):
        r0 = s * sub_rows

        def ucol(c0, w):
            p0 = c0 - c0 % _U_PIECE
            while p0 not in us[s]:
                emit(1)
            return us[s][p0][:, c0 - p0:c0 - p0 + w]

        z = jnp.dot(ucol(_A0, LANES).astype(BF16), wa2_ref[...], preferred_element_type=F32) + ba
        log_a = -(jnp.maximum(-z, 0.0) + jnp.log(1.0 + jnp.exp(-jnp.abs(z)))) * (1.0 / GATE_TAU)
        yield
        la1 = log_a.astype(BF16)
        rem = log_a - la1.astype(F32)
        la2 = rem.astype(BF16)
        la3 = (rem - la2.astype(F32)).astype(BF16)
        yield
        g_parts, glast_parts, decay = [], [], []
        for c in range(cpsub):
            sl = slice(c * chunk, (c + 1) * chunk)
            split = jnp.concatenate([la1[sl], la2[sl], la3[sl]], axis=0)
            g_c = jnp.dot(ltri, split, preferred_element_type=F32)
            gl = g_c[chunk - 1:chunk, :]
            g_parts.append(g_c)
            glast_parts.append(jnp.broadcast_to(gl, (chunk, D_K)))
            decay.append(jnp.exp(gl))
        g = jnp.concatenate(g_parts, axis=0)
        glast = jnp.concatenate(glast_parts, axis=0)
        g_mins.append(jnp.min(glast))
        yield
        q = ucol(_Q0, D_K) * (DK ** -0.5)
        k = ucol(_K0, D_K)
        qg = (q * jnp.exp(g)).astype(BF16)
        yield
        kgl = (k * jnp.exp(glast - g)).astype(BF16)
        yield
        if not direct:
            kng = (k * jnp.exp(-g)).astype(BF16)
            yield

        seg = min(tt, sub_rows)
        row = lax.broadcasted_iota(jnp.int32, (seg, _U_PIECE), 0)
        conv_halves, new_tail = [], {}
        for half in range(D_CONV // _U_PIECE):
            c_off = half * _U_PIECE
            uc = ucol(_GC0 + c_off, _U_PIECE) * ucol(_HC0 + c_off, _U_PIECE)
            conv_rows = []
            for i in range(sub_rows // seg):
                b = (r0 + i * seg) // tt
                ub = uc[i * seg:(i + 1) * seg]
                tl = tail[b][:, c_off:c_off + _U_PIECE]
                t1 = jnp.broadcast_to(tl[SUBLANES - 1:SUBLANES], (seg, _U_PIECE))
                t2 = jnp.broadcast_to(tl[SUBLANES - 2:SUBLANES - 1], (seg, _U_PIECE))
                u1 = jnp.where(row == 0, t1, pltpu.roll(ub, 1, 0))
                u2 = jnp.where(row == 0, t2, jnp.where(row == 1, t1, pltpu.roll(ub, 2, 0)))
                wch = wc[:, c_off:c_off + _U_PIECE]
                conv_rows.append(wch[0:1] * u2 + wch[1:2] * u1 + wch[2:3] * ub)
                new_tail.setdefault(b, []).append(ub[seg - SUBLANES:seg])
            conv_halves.append(ucol(_GB0 + c_off, _U_PIECE) * jnp.concatenate(conv_rows, axis=0))
            yield
        for b, parts in new_tail.items():
            tail[b] = jnp.concatenate(parts, axis=1)
        queue_out_proj(s, "conv", jnp.concatenate(conv_halves, axis=1).astype(BF16), D_GLA)

        sls = [slice(c * chunk, (c + 1) * chunk) for c in range(cpsub)]
        x_st, att, ds = [], [], []
        for c, sl in enumerate(sls):
            x_st.append(jnp.concatenate([ucol(_V0 + hh * DV, DV)[sl] for hh in range(N_HEADS)], axis=0))
            if direct:
                a_c = _direct_scores(q[sl], k[sl], g[sl], row_ref, chunk)
            else:
                a_c = lax.dot_general(qg[sl], _head_stack(kng[sl], lane_head), (((1,), (1,)), ((), ())),
                                      preferred_element_type=F32)
            att.append(a_c)
            yield
            ds.append(jnp.dot(x_st[c].T.astype(BF16), _head_stack(kgl[sl], lane_head), preferred_element_type=F32))
            yield
        o_st = []
        for c, sl in enumerate(sls):
            b = (r0 + c * chunk) // tt
            a_c = (att[c] * cmask).astype(BF16)
            o_c = jnp.dot(_head_stack(a_c, score_head), x_st[c].astype(BF16), preferred_element_type=F32)
            o_c = o_c + lax.dot_general(_head_stack(qg[sl], lane_head), state[b].astype(BF16),
                                        (((1,), (1,)), ((), ())), preferred_element_type=F32)
            o_st.append(o_c)
            state[b] = state[b] * decay[c] + ds[c]
            yield
        gla_rows = []
        for c, sl in enumerate(sls):
            r_st = jnp.concatenate([ucol(_R0 + hh * DV, DV)[sl] for hh in range(N_HEADS)], axis=0)
            ms = jnp.mean(o_st[c] * o_st[c], axis=-1, keepdims=True)
            o_c = o_st[c] * lax.rsqrt(ms + EPS) * g_head * _silu(r_st)
            gla_rows.append(jnp.concatenate([o_c[hh * chunk:(hh + 1) * chunk] for hh in range(N_HEADS)], axis=1))
            yield
        queue_out_proj(s, "gla", jnp.concatenate(gla_rows, axis=0).astype(BF16), 0)

    for s in range(n_sub):
        ys[s] = {}
        if s + 1 < n_sub:
            queue_in_proj(s + 1)
        for _ in stage(s):
            emit(1)
            side_step()
        if s > 0:
            finish(s - 1)
        emit(len(pending) - (d_out // _W_OUT_PIECE))

    emit(len(pending))
    finish(n_sub - 1)
    if side is not None:
        for _ in side:
            pass

    for b in range(nb):
        stn_ref[b] = state[b]
        ctn_ref[b] = tail[b]
    return functools.reduce(jnp.minimum, g_mins)


def _ffn_mix_body(x0_ref, xn_ref, s0_ref, c0_ref, n1_ref, wup_ref, wdn_ref, n2_ref, win_a_ref, win_b_ref, win_c_ref,
                  wa2_ref, ba_ref, gh_ref, wc_ref, wout_ref, ltri_ref, cmask_ref, o_ref, sout_ref, cout_ref,
                  st_ref, ct_ref, stn_ref, ctn_ref, row_ref, x1_ref,
                  *, nb, tt, tps, n_tiles, chunk, sub_rows, bcast_state):
    i = pl.program_id(0)
    t = lax.rem(i, tps)
    slot = lax.rem(i, 2)
    rows = nb * tt
    d = x0_ref.shape[-1]

    @pl.when(t == 0)
    def _load_state():
        for b in range(nb):
            sb = 0 if bcast_state else b
            st_ref[b] = s0_ref[sb].reshape(D_K, DV).T
            ct_ref[b] = c0_ref[sb]

    @pl.when(i == 0)
    def _first_ffn():
        out = []
        for _ in _ffn_items(x0_ref[...].reshape(rows, d), n1_ref, wup_ref, wdn_ref, out):
            pass
        x1_ref[0] = out[0]

    tile = functools.partial(
        _mix_tile, x1_ref.at[slot], n2_ref, (win_a_ref, win_b_ref, win_c_ref), wa2_ref, ba_ref, gh_ref, wc_ref,
        wout_ref, ltri_ref, cmask_ref, o_ref, st_ref, ct_ref, stn_ref, ctn_ref, row_ref, nb=nb, tt=tt, chunk=chunk, sub_rows=sub_rows)
    if n_tiles > 1:
        next_x1 = []
        g_min = tile(direct=False,
                     side=_ffn_items(xn_ref[...].reshape(rows, d), n1_ref, wup_ref, wdn_ref, next_x1))
        x1_ref[1 - slot] = next_x1[0]
    else:
        g_min = tile(direct=False)

    @pl.when(g_min < SAFE_LOG_DECAY)
    def _redo_without_factorisation():
        tile(direct=True)

    for b in range(nb):
        st_ref[b] = stn_ref[b]
        ct_ref[b] = ctn_ref[b]

    @pl.when(t == tps - 1)
    def _store_state():
        for b in range(nb):
            sout_ref[b] = stn_ref[b].T.reshape(N_HEADS, DK, DV)
            cout_ref[b] = ctn_ref[b]


def _ffn_mix(x, s0, c0, n1, wup_p, wdn_p, n2, win_a, win_b, win_c, wa2_p, ba, gh, wc, wout, *, nb, tt):
    bsz, seq, d = x.shape
    nb = min(nb, bsz)
    tt = min(tt, seq)
    assert bsz % nb == 0 and seq % tt == 0
    chunk = min(GLA_CHUNK, tt)
    sub_rows = min(MIX_SUB_ROWS, nb * tt)
    assert tt % chunk == 0 and (nb * tt) % sub_rows == 0
    assert (nb == 1 and tt % sub_rows == 0) or sub_rows % tt == 0
    bcast_state = s0.shape[0] != bsz
    assert not bcast_state or (s0.shape[0] == 1 and c0.shape[0] == 1)
    nbs = 1 if bcast_state else nb
    tps = seq // tt
    n_tiles = (bsz // nb) * tps
    state_idx = (lambda i: (0, 0, 0, 0)) if bcast_state else (lambda i: (i // tps, 0, 0, 0))
    tail_idx = (lambda i: (0, 0, 0)) if bcast_state else (lambda i: (i // tps, 0, 0))

    def nxt(i):
        return jnp.minimum(i + 1, n_tiles - 1)

    tri = np.tril(np.ones((chunk, chunk), np.float32))
    ltri = jnp.asarray(np.concatenate([tri, tri, tri], axis=1), dtype=BF16)
    cmask = jnp.asarray(np.tile(tri, (1, N_HEADS)), dtype=F32)

    body = functools.partial(_ffn_mix_body, nb=nb, tt=tt, tps=tps, n_tiles=n_tiles, chunk=chunk, sub_rows=sub_rows,
                             bcast_state=bcast_state)
    consts = (n1, wup_p, wdn_p, n2, win_a, win_b, win_c, wa2_p, ba, gh, wc, wout, ltri, cmask)
    return pl.pallas_call(
        body,
        grid=(n_tiles,),
        in_specs=[
            pl.BlockSpec((nb, tt, d), lambda i: (0, 0, 0), pipeline_mode=pl.Buffered(1)),
            pl.BlockSpec((nb, tt, d), lambda i: (nxt(i) // tps, nxt(i) % tps, 0)),
            pl.BlockSpec((nbs, N_HEADS, DK, DV), state_idx),
            pl.BlockSpec((nbs, SUBLANES, D_CONV), tail_idx),
        ] + [_const_spec(c.shape) for c in consts],
        out_specs=[
            pl.BlockSpec((nb, tt, d), lambda i: (i // tps, i % tps, 0)),
            pl.BlockSpec((nb, N_HEADS, DK, DV), lambda i: (i // tps, 0, 0, 0)),
            pl.BlockSpec((nb, SUBLANES, D_CONV), lambda i: (i // tps, 0, 0)),
        ],
        out_shape=[
            jax.ShapeDtypeStruct((bsz, seq, d), x.dtype),
            jax.ShapeDtypeStruct((bsz, N_HEADS, DK, DV), F32),
            jax.ShapeDtypeStruct((bsz, SUBLANES, D_CONV), F32),
        ],
        scratch_shapes=[
            pltpu.VMEM((nb, DV, D_K), F32),
            pltpu.VMEM((nb, SUBLANES, D_CONV), F32),
            pltpu.VMEM((nb, DV, D_K), F32),
            pltpu.VMEM((nb, SUBLANES, D_CONV), F32),
            pltpu.VMEM((2, chunk, D_K), F32),
            pltpu.VMEM((2, nb * tt, d), F32),
        ],
        compiler_params=pltpu.CompilerParams(
            dimension_semantics=("arbitrary",), vmem_limit_bytes=VMEM_LIMIT_BYTES),
        name="ffn_mix",
    )(x, x, s0, c0, *consts)


def _pack_mix(w_in, w_a2):
    lo = 2 * D_K + 2 * D_GLA
    hi = lo + GATE_RANK
    pad = LANES - GATE_RANK
    win_c = jnp.pad(w_in[:, lo:hi], ((0, 0), (0, pad))).astype(BF16)
    wa2_p = jnp.pad(w_a2, ((0, pad), (0, 0))).astype(BF16)
    return w_in[:, :lo].astype(BF16), w_in[:, hi:].astype(BF16), win_c, wa2_p


def kernel(x_prompt, x_sample, state_gla, cache_conv, meta, norm_ffn1, w_up1, w_down1, norm_mix, w_in, w_a2, b_a, g_head, w_conv, w_out, norm_ffn2, w_up2, w_down2, norm_final):
    d = x_prompt.shape[-1]
    seq_s = x_sample.shape[1]
    depth = w_in.shape[0]
    dt = x_prompt.dtype

    xm = meta.astype(dt)[None]
    xp, xs = x_prompt, x_sample
    sm = jnp.zeros((1, N_HEADS, DK, DV), F32)
    cm = jnp.zeros((1, SUBLANES, D_CONV), dt)
    nf = norm_final.reshape(1, d)
    sp_l, cp_l, ss_l, cs_l = [], [], [], []
    for i in range(depth):
        last = i == depth - 1
        wup1_p, wdn1_p = w_up1[i].astype(BF16), w_down1[i].astype(BF16)
        wup2_p, wdn2_p = w_up2[i].astype(BF16), w_down2[i].astype(BF16)
        win_a, win_b, win_c, wa2_p = _pack_mix(w_in[i], w_a2[i])
        lw = (norm_ffn1[i].reshape(1, d), wup1_p, wdn1_p,
              norm_mix[i].reshape(1, d), win_a, win_b, win_c, wa2_p, b_a[i].reshape(1, D_K), g_head[i].reshape(1, DV),
              w_conv[i], w_out[i].astype(BF16))
        n3 = norm_ffn2[i].reshape(1, d)
        cs0 = jnp.pad(cache_conv[i].astype(dt), ((0, 0), (SUBLANES - (CONV_W - 1), 0), (0, 0)))

        def ffn2(x):
            return _ffn(x.reshape(-1, d), n3, wup2_p, wdn2_p, nf, final_norm=last, rows=2 * TILE_ROWS).reshape(x.shape)

        xm, sm_new, cm_new = _ffn_mix(xm, sm, cm, *lw, nb=1, tt=meta.shape[0])
        xp, sp, cp = _ffn_mix(xp, sm_new, cm_new, *lw, nb=1, tt=TILE_ROWS)
        xs, ss, cs = _ffn_mix(xs, state_gla[i].astype(F32), cs0, *lw, nb=SAMPLE_TILE_SEQS, tt=seq_s)
        xp = ffn2(xp)
        xs = ffn2(xs)
        if not last:
            xm = ffn2(xm)
        sp_l.append(sp.astype(state_gla.dtype))
        cp_l.append(cp[:, SUBLANES - (CONV_W - 1):].astype(cache_conv.dtype))
        ss_l.append(ss.astype(state_gla.dtype))
        cs_l.append(cs[:, SUBLANES - (CONV_W - 1):].astype(cache_conv.dtype))
    return (xp, xs, jnp.stack(sp_l), jnp.stack(cp_l), jnp.stack(ss_l), jnp.stack(cs_l))
```

```python
import functools

import numpy as np
import jax
import jax.numpy as jnp
from jax import lax
from jax.experimental import pallas as pl
from jax.experimental.pallas import tpu as pltpu

EPS = 1e-6
N_HEADS = 4
DK = 64
DV = 128
D_K = N_HEADS * DK
D_GLA = N_HEADS * DV
D_CONV = 512
GATE_RANK = 16
GATE_TAU = 16.0
CONV_W = 3
GLA_CHUNK = 64
FF_CHUNK = 256
MIX_SUB_ROWS = 256
TILE_ROWS = 512
SAMPLE_TILE_SEQS = 4
SAFE_LOG_DECAY = -60.0
LANES = 128
SUBLANES = 8
VMEM_LIMIT_BYTES = 56 * 1024 * 1024

F32 = jnp.float32
BF16 = jnp.bfloat16


def _rms(x, g):
    ms = jnp.mean(x * x, axis=-1, keepdims=True)
    return x * lax.rsqrt(ms + EPS) * g


def _silu(x):
    return x * (1.0 / (1.0 + jnp.exp(-x)))


def _const_spec(shape):
    nd = len(shape)
    return pl.BlockSpec(shape, lambda *_: (0,) * nd, pipeline_mode=pl.Buffered(1))


def _ffn_items(x, n_ref, wup_ref, wdn_ref, out):
    h = _rms(x, n_ref[...]).astype(BF16)
    f = wdn_ref.shape[0]
    yield
    acc = None
    bounds = list(range(0, f, FF_CHUNK)) + [f]
    for c0, c1 in zip(bounds[:-1], bounds[1:]):
        gate = jnp.dot(h, wup_ref[:, c0:c1], preferred_element_type=F32)
        yield
        up = jnp.dot(h, wup_ref[:, f + c0:f + c1], preferred_element_type=F32)
        yield
        a = (_silu(gate) * up).astype(BF16)
        yield
        part = jnp.dot(a, wdn_ref[c0:c1, :], preferred_element_type=F32)
        acc = part if acc is None else acc + part
        yield
    out.append(x + 0.5 * acc)


def _ffn_body(x_ref, n_ref, wup_ref, wdn_ref, nf_ref, o_ref, *, final_norm):
    rows = x_ref.shape[0]
    n_streams = 2 if rows == 2 * TILE_ROWS else 1
    block = rows // n_streams
    outs = [[] for _ in range(n_streams)]
    gens = [_ffn_items(x_ref[k * block:(k + 1) * block, :], n_ref, wup_ref, wdn_ref, outs[k])
            for k in range(n_streams)]

    def finish(k):
        y = outs[k][0]
        if final_norm:
            y = _rms(y, nf_ref[...])
        o_ref[k * block:(k + 1) * block, :] = y

    lag = 2 * -(-wdn_ref.shape[0] // FF_CHUNK)
    done = [False] * n_streams
    step = 0
    while not all(done):
        for k in range(n_streams):
            if done[k] or step < k * lag:
                continue
            try:
                next(gens[k])
            except StopIteration:
                done[k] = True
                finish(k)
        step += 1


def _ffn(x2d, n, wup_p, wdn_p, nf, *, final_norm, rows):
    m, d = x2d.shape
    rows = min(rows, m)
    assert m % rows == 0 and wdn_p.shape[0] % LANES == 0
    body = functools.partial(_ffn_body, final_norm=final_norm)
    return pl.pallas_call(
        body,
        grid=(m // rows,),
        in_specs=[
            pl.BlockSpec((rows, d), lambda i: (i, 0)),
            _const_spec(n.shape),
            _const_spec(wup_p.shape),
            _const_spec(wdn_p.shape),
            _const_spec(nf.shape),
        ],
        out_specs=pl.BlockSpec((rows, d), lambda i: (i, 0)),
        out_shape=jax.ShapeDtypeStruct((m, d), x2d.dtype),
        compiler_params=pltpu.CompilerParams(
            dimension_semantics=("parallel",), vmem_limit_bytes=VMEM_LIMIT_BYTES),
        name="ffn_final" if final_norm else "ffn",
    )(x2d, n, wup_p, wdn_p, nf)


def _head_stack(a, head_of_lane):
    zero = jnp.zeros((), a.dtype)
    return jnp.concatenate([jnp.where(head_of_lane == hh, a, zero) for hh in range(N_HEADS)], axis=0)


def _direct_scores(q_c, k_c, g_c, row_ref, chunk):
    hc = N_HEADS * chunk
    row_ref[0] = k_c
    row_ref[1] = g_c
    lane_head = lax.broadcasted_iota(jnp.int32, (1, D_K), 1) // DK
    score_head = lax.broadcasted_iota(jnp.int32, (1, hc), 1) // chunk
    key_of_lane = lax.broadcasted_iota(jnp.int32, (chunk, hc), 1) % chunk

    def one_key(j, att):
        kj = row_ref[0, pl.ds(j, 1), :]
        gj = row_ref[1, pl.ds(j, 1), :]
        p = q_c * kj * jnp.exp(jnp.minimum(g_c - gj, 0.0))
        col = jnp.zeros((chunk, hc), F32)
        for hh in range(N_HEADS):
            s_h = jnp.sum(jnp.where(lane_head == hh, p, 0.0), axis=-1, keepdims=True)
            col = jnp.where(score_head == hh, s_h, col)
        return jnp.where(key_of_lane == j, col, att)

    return lax.fori_loop(0, chunk, one_key, jnp.zeros((chunk, hc), F32))


_Q0, _K0, _V0 = 0, D_K, 2 * D_K
_R0 = _V0 + D_GLA
_GB0 = _R0 + D_GLA
_GC0 = _GB0 + D_CONV
_HC0 = _GC0 + D_CONV
_A0 = _HC0 + D_CONV
_U_WIDTH = _A0 + LANES
_U_PIECE = 256
_U_ORDER = (_A0, _Q0, _K0, _GC0, _HC0, _GB0, _GC0 + _U_PIECE, _HC0 + _U_PIECE, _GB0 + _U_PIECE,
            _V0, _V0 + _U_PIECE, _R0, _R0 + _U_PIECE)
_W_OUT_PIECE = 512


def _mix_tile(x_ref, n_ref, win_ref, wa2_ref, ba_ref, gh_ref, wc_ref, wout_ref, ltri_ref, cmask_ref,
              o_ref, st_ref, ct_ref, stn_ref, ctn_ref, row_ref, *, nb, tt, chunk, sub_rows, direct, side=None):
    rows = nb * tt
    n_sub = rows // sub_rows
    cpsub = sub_rows // chunk
    hc = N_HEADS * chunk
    d = x_ref.shape[-1]

    lane_head = lax.broadcasted_iota(jnp.int32, (1, D_K), 1) // DK
    score_head = lax.broadcasted_iota(jnp.int32, (1, hc), 1) // chunk
    ltri = ltri_ref[...]
    cmask = cmask_ref[...]
    g_head = gh_ref[...]
    wc = wc_ref[...]
    norm_g = n_ref[...]
    ba = ba_ref[...]

    state = {b: st_ref[b] for b in range(nb)}
    tail = {b: ct_ref[b] for b in range(nb)}
    g_mins = []

    def load_x(s):
        return x_ref[s * sub_rows:(s + 1) * sub_rows, :]

    def store_y(s, y):
        r0 = s * sub_rows
        if tt >= sub_rows:
            b0, t0 = divmod(r0, tt)
            o_ref[b0, t0:t0 + sub_rows, :] = y
        else:
            o_ref[r0 // tt:(r0 + sub_rows) // tt] = y.reshape(sub_rows // tt, tt, d)

    xs, us, ys = {}, {}, {}
    pending = []

    def piece_width(c0):
        return min(_U_PIECE, _U_WIDTH - c0)

    def queue_in_proj(s):
        xs[s] = load_x(s)
        h = _rms(xs[s], norm_g).astype(BF16)
        us[s] = {}

        def piece(c0):
            ref, off = (win_ref[0], 0) if c0 < _GB0 else (win_ref[1], _GB0) if c0 < _A0 else (win_ref[2], _A0)
            us[s][c0] = jnp.dot(h, ref[:, c0 - off:c0 - off + piece_width(c0)], preferred_element_type=F32)
        pending.extend(functools.partial(piece, c0) for c0 in _U_ORDER)

    def queue_out_proj(s, part, lhs, k0):
        def piece(j):
            ys[s][part].append(jnp.dot(lhs, wout_ref[k0:k0 + lhs.shape[1], j * _W_OUT_PIECE:(j + 1) * _W_OUT_PIECE],
                                       preferred_element_type=F32))
        ys[s][part] = []
        pending.extend(functools.partial(piece, j) for j in range(wout_ref.shape[1] // _W_OUT_PIECE))

    def emit(k):
        for _ in range(min(k, len(pending))):
            pending.pop(0)()

    def side_step():
        if side is not None:
            next(side, None)

    def finish(s):
        y = xs[s] + jnp.concatenate(ys[s]["conv"], axis=1) + jnp.concatenate(ys[s]["gla"], axis=1)
        store_y(s, y)

    d_out = wout_ref.shape[1]
    queue_in_proj(0)
    for _ in range(3):
        emit(1)
        side_step()

    def stage(s):
        r0 = s * sub_rows

        def ucol(c0, w):
            p0 = c0 - c0 % _U_PIECE
            while p0 not in us[s]:
                emit(1)
            return us[s][p0][:, c0 - p0:c0 - p0 + w]

        z = jnp.dot(ucol(_A0, LANES).astype(BF16), wa2_ref[...], preferred_element_type=F32) + ba
        log_a = -(jnp.maximum(-z, 0.0) + jnp.log(1.0 + jnp.exp(-jnp.abs(z)))) * (1.0 / GATE_TAU)
        yield
        la1 = log_a.astype(BF16)
        rem = log_a - la1.astype(F32)
        la2 = rem.astype(BF16)
        la3 = (rem - la2.astype(F32)).astype(BF16)
        yield
        g_parts, glast_parts, decay = [], [], []
        for c in range(cpsub):
            sl = slice(c * chunk, (c + 1) * chunk)
            split = jnp.concatenate([la1[sl], la2[sl], la3[sl]], axis=0)
            g_c = jnp.dot(ltri, split, preferred_element_type=F32)
            gl = g_c[chunk - 1:chunk, :]
            g_parts.append(g_c)
            glast_parts.append(jnp.broadcast_to(gl, (chunk, D_K)))
            decay.append(jnp.exp(gl))
        g = jnp.concatenate(g_parts, axis=0)
        glast = jnp.concatenate(glast_parts, axis=0)
        g_mins.append(jnp.min(glast))
        yield
        q = ucol(_Q0, D_K) * (DK ** -0.5)
        k = ucol(_K0, D_K)
        qg = (q * jnp.exp(g)).astype(BF16)
        yield
        kgl = (k * jnp.exp(glast - g)).astype(BF16)
        yield
        if not direct:
            kng = (k * jnp.exp(-g)).astype(BF16)
            yield

        seg = min(tt, sub_rows)
        row = lax.broadcasted_iota(jnp.int32, (seg, _U_PIECE), 0)
        conv_halves, new_tail = [], {}
        for half in range(D_CONV // _U_PIECE):
            c_off = half * _U_PIECE
            uc = ucol(_GC0 + c_off, _U_PIECE) * ucol(_HC0 + c_off, _U_PIECE)
            conv_rows = []
            for i in range(sub_rows // seg):
                b = (r0 + i * seg) // tt
                ub = uc[i * seg:(i + 1) * seg]
                tl = tail[b][:, c_off:c_off + _U_PIECE]
                t1 = jnp.broadcast_to(tl[SUBLANES - 1:SUBLANES], (seg, _U_PIECE))
                t2 = jnp.broadcast_to(tl[SUBLANES - 2:SUBLANES - 1], (seg, _U_PIECE))
                u1 = jnp.where(row == 0, t1, pltpu.roll(ub, 1, 0))
                u2 = jnp.where(row == 0, t2, jnp.where(row == 1, t1, pltpu.roll(ub, 2, 0)))
                wch = wc[:, c_off:c_off + _U_PIECE]
                conv_rows.append(wch[0:1] * u2 + wch[1:2] * u1 + wch[2:3] * ub)
                new_tail.setdefault(b, []).append(ub[seg - SUBLANES:seg])
            conv_halves.append(ucol(_GB0 + c_off, _U_PIECE) * jnp.concatenate(conv_rows, axis=0))
            yield
        for b, parts in new_tail.items():
            tail[b] = jnp.concatenate(parts, axis=1)
        queue_out_proj(s, "conv", jnp.concatenate(conv_halves, axis=1).astype(BF16), D_GLA)

        sls = [slice(c * chunk, (c + 1) * chunk) for c in range(cpsub)]
        x_st, att, ds = [], [], []
        for c, sl in enumerate(sls):
            x_st.append(jnp.concatenate([ucol(_V0 + hh * DV, DV)[sl] for hh in range(N_HEADS)], axis=0))
            if direct:
                a_c = _direct_scores(q[sl], k[sl], g[sl], row_ref, chunk)
            else:
                a_c = lax.dot_general(qg[sl], _head_stack(kng[sl], lane_head), (((1,), (1,)), ((), ())),
                                      preferred_element_type=F32)
            att.append(a_c)
            yield
            ds.append(jnp.dot(x_st[c].T.astype(BF16), _head_stack(kgl[sl], lane_head), preferred_element_type=F32))
            yield
        o_st = []
        for c, sl in enumerate(sls):
            b = (r0 + c * chunk) // tt
            a_c = (att[c] * cmask).astype(BF16)
            o_c = jnp.dot(_head_stack(a_c, score_head), x_st[c].astype(BF16), preferred_element_type=F32)
            o_c = o_c + lax.dot_general(_head_stack(qg[sl], lane_head), state[b].astype(BF16),
                                        (((1,), (1,)), ((), ())), preferred_element_type=F32)
            o_st.append(o_c)
            state[b] = state[b] * decay[c] + ds[c]
            yield
        gla_rows = []
        for c, sl in enumerate(sls):
            r_st = jnp.concatenate([ucol(_R0 + hh * DV, DV)[sl] for hh in range(N_HEADS)], axis=0)
            ms = jnp.mean(o_st[c] * o_st[c], axis=-1, keepdims=True)
            o_c = o_st[c] * lax.rsqrt(ms + EPS) * g_head * _silu(r_st)
            gla_rows.append(jnp.concatenate([o_c[hh * chunk:(hh + 1) * chunk] for hh in range(N_HEADS)], axis=1))
            yield
        queue_out_proj(s, "gla", jnp.concatenate(gla_rows, axis=0).astype(BF16), 0)

    for s in range(n_sub):
        ys[s] = {}
        if s + 1 < n_sub:
            queue_in_proj(s + 1)
        for _ in stage(s):
            emit(1)
            side_step()
        if s > 0:
            finish(s - 1)
        emit(len(pending) - (d_out // _W_OUT_PIECE))

    emit(len(pending))
    finish(n_sub - 1)
    if side is not None:
        for _ in side:
            pass

    for b in range(nb):
        stn_ref[b] = state[b]
        ctn_ref[b] = tail[b]
    return functools.reduce(jnp.minimum, g_mins)


def _ffn_mix_body(x0_ref, xn_ref, s0_ref, c0_ref, n1_ref, wup_ref, wdn_ref, n2_ref, win_a_ref, win_b_ref, win_c_ref,
                  wa2_ref, ba_ref, gh_ref, wc_ref, wout_ref, ltri_ref, cmask_ref, o_ref, sout_ref, cout_ref,
                  st_ref, ct_ref, stn_ref, ctn_ref, row_ref, x1_ref,
                  *, nb, tt, tps, n_tiles, chunk, sub_rows, bcast_state):
    i = pl.program_id(0)
    t = lax.rem(i, tps)
    slot = lax.rem(i, 2)
    rows = nb * tt
    d = x0_ref.shape[-1]

    @pl.when(t == 0)
    def _load_state():
        for b in range(nb):
            sb = 0 if bcast_state else b
            st_ref[b] = s0_ref[sb].reshape(D_K, DV).T
            ct_ref[b] = c0_ref[sb]

    @pl.when(i == 0)
    def _first_ffn():
        out = []
        for _ in _ffn_items(x0_ref[...].reshape(rows, d), n1_ref, wup_ref, wdn_ref, out):
            pass
        x1_ref[0] = out[0]

    tile = functools.partial(
        _mix_tile, x1_ref.at[slot], n2_ref, (win_a_ref, win_b_ref, win_c_ref), wa2_ref, ba_ref, gh_ref, wc_ref,
        wout_ref, ltri_ref, cmask_ref, o_ref, st_ref, ct_ref, stn_ref, ctn_ref, row_ref, nb=nb, tt=tt, chunk=chunk, sub_rows=sub_rows)
    if n_tiles > 1:
        next_x1 = []
        g_min = tile(direct=False,
                     side=_ffn_items(xn_ref[...].reshape(rows, d), n1_ref, wup_ref, wdn_ref, next_x1))
        x1_ref[1 - slot] = next_x1[0]
    else:
        g_min = tile(direct=False)

    @pl.when(g_min < SAFE_LOG_DECAY)
    def _redo_without_factorisation():
        tile(direct=True)

    for b in range(nb):
        st_ref[b] = stn_ref[b]
        ct_ref[b] = ctn_ref[b]

    @pl.when(t == tps - 1)
    def _store_state():
        for b in range(nb):
            sout_ref[b] = stn_ref[b].T.reshape(N_HEADS, DK, DV)
            cout_ref[b] = ctn_ref[b]


def _ffn_mix(x, s0, c0, n1, wup_p, wdn_p, n2, win_a, win_b, win_c, wa2_p, ba, gh, wc, wout, *, nb, tt):
    bsz, seq, d = x.shape
    nb = min(nb, bsz)
    tt = min(tt, seq)
    assert bsz % nb == 0 and seq % tt == 0
    chunk = min(GLA_CHUNK, tt)
    sub_rows = min(MIX_SUB_ROWS, nb * tt)
    assert tt % chunk == 0 and (nb * tt) % sub_rows == 0
    assert (nb == 1 and tt % sub_rows == 0) or sub_rows % tt == 0
    bcast_state = s0.shape[0] != bsz
    assert not bcast_state or (s0.shape[0] == 1 and c0.shape[0] == 1)
    nbs = 1 if bcast_state else nb
    tps = seq // tt
    n_tiles = (bsz // nb) * tps
    state_idx = (lambda i: (0, 0, 0, 0)) if bcast_state else (lambda i: (i // tps, 0, 0, 0))
    tail_idx = (lambda i: (0, 0, 0)) if bcast_state else (lambda i: (i // tps, 0, 0))

    def nxt(i):
        return jnp.minimum(i + 1, n_tiles - 1)

    tri = np.tril(np.ones((chunk, chunk), np.float32))
    ltri = jnp.asarray(np.concatenate([tri, tri, tri], axis=1), dtype=BF16)
    cmask = jnp.asarray(np.tile(tri, (1, N_HEADS)), dtype=F32)

    body = functools.partial(_ffn_mix_body, nb=nb, tt=tt, tps=tps, n_tiles=n_tiles, chunk=chunk, sub_rows=sub_rows,
                             bcast_state=bcast_state)
    consts = (n1, wup_p, wdn_p, n2, win_a, win_b, win_c, wa2_p, ba, gh, wc, wout, ltri, cmask)
    return pl.pallas_call(
        body,
        grid=(n_tiles,),
        in_specs=[
            pl.BlockSpec((nb, tt, d), lambda i: (0, 0, 0), pipeline_mode=pl.Buffered(1)),
            pl.BlockSpec((nb, tt, d), lambda i: (nxt(i) // tps, nxt(i) % tps, 0)),
            pl.BlockSpec((nbs, N_HEADS, DK, DV), state_idx),
            pl.BlockSpec((nbs, SUBLANES, D_CONV), tail_idx),
        ] + [_const_spec(c.shape) for c in consts],
        out_specs=[
            pl.BlockSpec((nb, tt, d), lambda i: (i // tps, i % tps, 0)),
            pl.BlockSpec((nb, N_HEADS, DK, DV), lambda i: (i // tps, 0, 0, 0)),
            pl.BlockSpec((nb, SUBLANES, D_CONV), lambda i: (i // tps, 0, 0)),
        ],
        out_shape=[
            jax.ShapeDtypeStruct((bsz, seq, d), x.dtype),
            jax.ShapeDtypeStruct((bsz, N_HEADS, DK, DV), F32),
            jax.ShapeDtypeStruct((bsz, SUBLANES, D_CONV), F32),
        ],
        scratch_shapes=[
            pltpu.VMEM((nb, DV, D_K), F32),
            pltpu.VMEM((nb, SUBLANES, D_CONV), F32),
            pltpu.VMEM((nb, DV, D_K), F32),
            pltpu.VMEM((nb, SUBLANES, D_CONV), F32),
            pltpu.VMEM((2, chunk, D_K), F32),
            pltpu.VMEM((2, nb * tt, d), F32),
        ],
        compiler_params=pltpu.CompilerParams(
            dimension_semantics=("arbitrary",), vmem_limit_bytes=VMEM_LIMIT_BYTES),
        name="ffn_mix",
    )(x, x, s0, c0, *consts)


def _pack_mix(w_in, w_a2):
    lo = 2 * D_K + 2 * D_GLA
    hi = lo + GATE_RANK
    pad = LANES - GATE_RANK
    win_c = jnp.pad(w_in[:, lo:hi], ((0, 0), (0, pad))).astype(BF16)
    wa2_p = jnp.pad(w_a2, ((0, pad), (0, 0))).astype(BF16)
    return w_in[:, :lo].astype(BF16), w_in[:, hi:].astype(BF16), win_c, wa2_p


def kernel(x_prompt, x_sample, state_gla, cache_conv, meta, norm_ffn1, w_up1, w_down1, norm_mix, w_in, w_a2, b_a, g_head, w_conv, w_out, norm_ffn2, w_up2, w_down2, norm_final):
    d = x_prompt.shape[-1]
    seq_s = x_sample.shape[1]
    depth = w_in.shape[0]
    dt = x_prompt.dtype

    xm = meta.astype(dt)[None]
    xp, xs = x_prompt, x_sample
    sm = jnp.zeros((1, N_HEADS, DK, DV), F32)
    cm = jnp.zeros((1, SUBLANES, D_CONV), dt)
    nf = norm_final.reshape(1, d)
    sp_l, cp_l, ss_l, cs_l = [], [], [], []
    for i in range(depth):
        last = i == depth - 1
        wup1_p, wdn1_p = w_up1[i].astype(BF16), w_down1[i].astype(BF16)
        wup2_p, wdn2_p = w_up2[i].astype(BF16), w_down2[i].astype(BF16)
        win_a, win_b, win_c, wa2_p = _pack_mix(w_in[i], w_a2[i])
        lw = (norm_ffn1[i].reshape(1, d), wup1_p, wdn1_p,
              norm_mix[i].reshape(1, d), win_a, win_b, win_c, wa2_p, b_a[i].reshape(1, D_K), g_head[i].reshape(1, DV),
              w_conv[i], w_out[i].astype(BF16))
        n3 = norm_ffn2[i].reshape(1, d)
        cs0 = jnp.pad(cache_conv[i].astype(dt), ((0, 0), (SUBLANES - (CONV_W - 1), 0), (0, 0)))

        def ffn2(x):
            return _ffn(x.reshape(-1, d), n3, wup2_p, wdn2_p, nf, final_norm=last, rows=2 * TILE_ROWS).reshape(x.shape)

        xm, sm_new, cm_new = _ffn_mix(xm, sm, cm, *lw, nb=1, tt=meta.shape[0])
        xp, sp, cp = _ffn_mix(xp, sm_new, cm_new, *lw, nb=1, tt=TILE_ROWS)
        xs, ss, cs = _ffn_mix(xs, state_gla[i].astype(F32), cs0, *lw, nb=SAMPLE_TILE_SEQS, tt=seq_s)
        xp = ffn2(xp)
        xs = ffn2(xs)
        if not last:
            xm = ffn2(xm)
        sp_l.append(sp.astype(state_gla.dtype))
        cp_l.append(cp[:, SUBLANES - (CONV_W - 1):].astype(cache_conv.dtype))
        ss_l.append(ss.astype(state_gla.dtype))
        cs_l.append(cs[:, SUBLANES - (CONV_W - 1):].astype(cache_conv.dtype))
    return (xp, xs, jnp.stack(sp_l), jnp.stack(cp_l), jnp.stack(ss_l), jnp.stack(cs_l))
```

```python
import functools

import numpy as np
import jax
import jax.numpy as jnp
from jax import lax
from jax.experimental import pallas as pl
from jax.experimental.pallas import tpu as pltpu

EPS = 1e-6
N_HEADS = 4
DK = 64
DV = 128
D_K = N_HEADS * DK
D_GLA = N_HEADS * DV
D_CONV = 512
GATE_RANK = 16
GATE_TAU = 16.0
CONV_W = 3
GLA_CHUNK = 64
FF_CHUNK = 256
MIX_SUB_ROWS = 256
TILE_ROWS = 512
SAMPLE_TILE_SEQS = 4
SAFE_LOG_DECAY = -60.0
LANES = 128
SUBLANES = 8
VMEM_LIMIT_BYTES = 56 * 1024 * 1024

F32 = jnp.float32
BF16 = jnp.bfloat16


def _rms(x, g):
    ms = jnp.mean(x * x, axis=-1, keepdims=True)
    return x * lax.rsqrt(ms + EPS) * g


def _silu(x):
    return x * (1.0 / (1.0 + jnp.exp(-x)))


def _const_spec(shape):
    nd = len(shape)
    return pl.BlockSpec(shape, lambda *_: (0,) * nd, pipeline_mode=pl.Buffered(1))


def _ffn_items(x, n_ref, wup_ref, wdn_ref, out):
    h = _rms(x, n_ref[...]).astype(BF16)
    f = wdn_ref.shape[0]
    yield
    acc = None
    for c0 in range(0, f, FF_CHUNK):
        gate = jnp.dot(h, wup_ref[:, c0:c0 + FF_CHUNK], preferred_element_type=F32)
        yield
        up = jnp.dot(h, wup_ref[:, f + c0:f + c0 + FF_CHUNK], preferred_element_type=F32)
        yield
        a = (_silu(gate) * up).astype(BF16)
        yield
        part = jnp.dot(a, wdn_ref[c0:c0 + FF_CHUNK, :], preferred_element_type=F32)
        acc = part if acc is None else acc + part
        yield
    out.append(x + 0.5 * acc)


def _ffn_body(x_ref, n_ref, wup_ref, wdn_ref, nf_ref, o_ref, *, final_norm):
    out = []
    for _ in _ffn_items(x_ref[...], n_ref, wup_ref, wdn_ref, out):
        pass
    y = out[0]
    if final_norm:
        y = _rms(y, nf_ref[...])
    o_ref[...] = y


def _ffn(x2d, n, wup_p, wdn_p, nf, *, final_norm, rows):
    m, d = x2d.shape
    rows = min(rows, m)
    assert m % rows == 0 and wdn_p.shape[0] % FF_CHUNK == 0
    body = functools.partial(_ffn_body, final_norm=final_norm)
    return pl.pallas_call(
        body,
        grid=(m // rows,),
        in_specs=[
            pl.BlockSpec((rows, d), lambda i: (i, 0)),
            _const_spec(n.shape),
            _const_spec(wup_p.shape),
            _const_spec(wdn_p.shape),
            _const_spec(nf.shape),
        ],
        out_specs=pl.BlockSpec((rows, d), lambda i: (i, 0)),
        out_shape=jax.ShapeDtypeStruct((m, d), x2d.dtype),
        compiler_params=pltpu.CompilerParams(
            dimension_semantics=("parallel",), vmem_limit_bytes=VMEM_LIMIT_BYTES),
        name="ffn_final" if final_norm else "ffn",
    )(x2d, n, wup_p, wdn_p, nf)


def _head_stack(a, head_of_lane):
    zero = jnp.zeros((), a.dtype)
    return jnp.concatenate([jnp.where(head_of_lane == hh, a, zero) for hh in range(N_HEADS)], axis=0)


def _direct_scores(q_c, k_c, g_c, row_ref, chunk):
    hc = N_HEADS * chunk
    row_ref[0] = k_c
    row_ref[1] = g_c
    lane_head = lax.broadcasted_iota(jnp.int32, (1, D_K), 1) // DK
    score_head = lax.broadcasted_iota(jnp.int32, (1, hc), 1) // chunk
    key_of_lane = lax.broadcasted_iota(jnp.int32, (chunk, hc), 1) % chunk

    def one_key(j, att):
        kj = row_ref[0, pl.ds(j, 1), :]
        gj = row_ref[1, pl.ds(j, 1), :]
        p = q_c * kj * jnp.exp(jnp.minimum(g_c - gj, 0.0))
        col = jnp.zeros((chunk, hc), F32)
        for hh in range(N_HEADS):
            s_h = jnp.sum(jnp.where(lane_head == hh, p, 0.0), axis=-1, keepdims=True)
            col = jnp.where(score_head == hh, s_h, col)
        return jnp.where(key_of_lane == j, col, att)

    return lax.fori_loop(0, chunk, one_key, jnp.zeros((chunk, hc), F32))


_Q0, _K0, _V0 = 0, D_K, 2 * D_K
_R0 = _V0 + D_GLA
_GB0 = _R0 + D_GLA
_GC0 = _GB0 + D_CONV
_HC0 = _GC0 + D_CONV
_A0 = _HC0 + D_CONV
_U_WIDTH = _A0 + LANES
_U_PIECE = 256
_U_ORDER = (_A0, _Q0, _K0, _GC0, _HC0, _GB0, _GC0 + _U_PIECE, _HC0 + _U_PIECE, _GB0 + _U_PIECE,
            _V0, _V0 + _U_PIECE, _R0, _R0 + _U_PIECE)
_W_OUT_PIECE = 512
SIDE_DOUBLE_FROM_STEP = 16


def _mix_tile(x_ref, n_ref, win_ref, wa2_ref, ba_ref, gh_ref, wc_ref, wout_ref, ltri_ref, cmask_ref,
              o_ref, st_ref, ct_ref, stn_ref, ctn_ref, row_ref, *, nb, tt, chunk, sub_rows, direct, side=None):
    rows = nb * tt
    n_sub = rows // sub_rows
    cpsub = sub_rows // chunk
    hc = N_HEADS * chunk
    d = x_ref.shape[-1]

    lane_head = lax.broadcasted_iota(jnp.int32, (1, D_K), 1) // DK
    score_head = lax.broadcasted_iota(jnp.int32, (1, hc), 1) // chunk
    ltri = ltri_ref[...]
    cmask = cmask_ref[...]
    g_head = gh_ref[...]
    wc = wc_ref[...]
    norm_g = n_ref[...]
    ba = ba_ref[...]

    state = {b: st_ref[b] for b in range(nb)}
    tail = {b: ct_ref[b] for b in range(nb)}
    g_mins = []

    def load_x(s):
        return x_ref[s * sub_rows:(s + 1) * sub_rows, :]

    def store_y(s, y):
        r0 = s * sub_rows
        if tt >= sub_rows:
            b0, t0 = divmod(r0, tt)
            o_ref[b0, t0:t0 + sub_rows, :] = y
        else:
            o_ref[r0 // tt:(r0 + sub_rows) // tt] = y.reshape(sub_rows // tt, tt, d)

    xs, us, ys = {}, {}, {}
    pending = []

    def piece_width(c0):
        return min(_U_PIECE, _U_WIDTH - c0)

    def queue_in_proj(s):
        xs[s] = load_x(s)
        h = _rms(xs[s], norm_g).astype(BF16)
        us[s] = {}

        def piece(c0):
            ref, off = (win_ref[0], 0) if c0 < _GB0 else (win_ref[1], _GB0) if c0 < _A0 else (win_ref[2], _A0)
            us[s][c0] = jnp.dot(h, ref[:, c0 - off:c0 - off + piece_width(c0)], preferred_element_type=F32)
        pending.extend(functools.partial(piece, c0) for c0 in _U_ORDER)

    def queue_out_proj(s, part, lhs, k0):
        def piece(j):
            ys[s][part].append(jnp.dot(lhs, wout_ref[k0:k0 + lhs.shape[1], j * _W_OUT_PIECE:(j + 1) * _W_OUT_PIECE],
                                       preferred_element_type=F32))
        ys[s][part] = []
        pending.extend(functools.partial(piece, j) for j in range(wout_ref.shape[1] // _W_OUT_PIECE))

    def emit(k):
        for _ in range(min(k, len(pending))):
            pending.pop(0)()

    def side_step():
        if side is not None:
            next(side, None)

    def finish(s):
        y = xs[s] + jnp.concatenate(ys[s]["conv"], axis=1) + jnp.concatenate(ys[s]["gla"], axis=1)
        store_y(s, y)

    d_out = wout_ref.shape[1]
    queue_in_proj(0)
    for _ in range(3):
        emit(1)
        side_step()

    def stage(s):
        r0 = s * sub_rows

        def ucol(c0, w):
            p0 = c0 - c0 % _U_PIECE
            while p0 not in us[s]:
                emit(1)
            return us[s][p0][:, c0 - p0:c0 - p0 + w]

        z = jnp.dot(ucol(_A0, LANES).astype(BF16), wa2_ref[...], preferred_element_type=F32) + ba
        log_a = -(jnp.maximum(-z, 0.0) + jnp.log(1.0 + jnp.exp(-jnp.abs(z)))) * (1.0 / GATE_TAU)
        yield
        la1 = log_a.astype(BF16)
        rem = log_a - la1.astype(F32)
        la2 = rem.astype(BF16)
        la3 = (rem - la2.astype(F32)).astype(BF16)
        yield
        g_parts, glast_parts, decay = [], [], []
        for c in range(cpsub):
            sl = slice(c * chunk, (c + 1) * chunk)
            split = jnp.concatenate([la1[sl], la2[sl], la3[sl]], axis=0)
            g_c = jnp.dot(ltri, split, preferred_element_type=F32)
            gl = g_c[chunk - 1:chunk, :]
            g_parts.append(g_c)
            glast_parts.append(jnp.broadcast_to(gl, (chunk, D_K)))
            decay.append(jnp.exp(gl))
        g = jnp.concatenate(g_parts, axis=0)
        glast = jnp.concatenate(glast_parts, axis=0)
        g_mins.append(jnp.min(glast))
        yield
        q = ucol(_Q0, D_K) * (DK ** -0.5)
        k = ucol(_K0, D_K)
        qg = (q * jnp.exp(g)).astype(BF16)
        yield
        kgl = (k * jnp.exp(glast - g)).astype(BF16)
        yield
        if not direct:
            kng = (k * jnp.exp(-g)).astype(BF16)
            yield

        seg = min(tt, sub_rows)
        row = lax.broadcasted_iota(jnp.int32, (seg, _U_PIECE), 0)
        conv_halves, new_tail = [], {}
        for half in range(D_CONV // _U_PIECE):
            c_off = half * _U_PIECE
            uc = ucol(_GC0 + c_off, _U_PIECE) * ucol(_HC0 + c_off, _U_PIECE)
            conv_rows = []
            for i in range(sub_rows // seg):
                b = (r0 + i * seg) // tt
                ub = uc[i * seg:(i + 1) * seg]
                tl = tail[b][:, c_off:c_off + _U_PIECE]
                t1 = jnp.broadcast_to(tl[SUBLANES - 1:SUBLANES], (seg, _U_PIECE))
                t2 = jnp.broadcast_to(tl[SUBLANES - 2:SUBLANES - 1], (seg, _U_PIECE))
                u1 = jnp.where(row == 0, t1, pltpu.roll(ub, 1, 0))
                u2 = jnp.where(row == 0, t2, jnp.where(row == 1, t1, pltpu.roll(ub, 2, 0)))
                wch = wc[:, c_off:c_off + _U_PIECE]
                conv_rows.append(wch[0:1] * u2 + wch[1:2] * u1 + wch[2:3] * ub)
                new_tail.setdefault(b, []).append(ub[seg - SUBLANES:seg])
            conv_halves.append(ucol(_GB0 + c_off, _U_PIECE) * jnp.concatenate(conv_rows, axis=0))
            yield
        for b, parts in new_tail.items():
            tail[b] = jnp.concatenate(parts, axis=1)
        queue_out_proj(s, "conv", jnp.concatenate(conv_halves, axis=1).astype(BF16), D_GLA)

        sls = [slice(c * chunk, (c + 1) * chunk) for c in range(cpsub)]
        x_st, att, ds = [], [], []
        for c, sl in enumerate(sls):
            x_st.append(jnp.concatenate([ucol(_V0 + hh * DV, DV)[sl] for hh in range(N_HEADS)], axis=0))
            if direct:
                a_c = _direct_scores(q[sl], k[sl], g[sl], row_ref, chunk)
            else:
                a_c = lax.dot_general(qg[sl], _head_stack(kng[sl], lane_head), (((1,), (1,)), ((), ())),
                                      preferred_element_type=F32)
            att.append(a_c)
            yield
            ds.append(jnp.dot(x_st[c].T.astype(BF16), _head_stack(kgl[sl], lane_head), preferred_element_type=F32))
            yield
        o_st = []
        for c, sl in enumerate(sls):
            b = (r0 + c * chunk) // tt
            a_c = (att[c] * cmask).astype(BF16)
            o_c = jnp.dot(_head_stack(a_c, score_head), x_st[c].astype(BF16), preferred_element_type=F32)
            o_c = o_c + lax.dot_general(_head_stack(qg[sl], lane_head), state[b].astype(BF16),
                                        (((1,), (1,)), ((), ())), preferred_element_type=F32)
            o_st.append(o_c)
            state[b] = state[b] * decay[c] + ds[c]
            yield
        gla_rows = []
        for c, sl in enumerate(sls):
            r_st = jnp.concatenate([ucol(_R0 + hh * DV, DV)[sl] for hh in range(N_HEADS)], axis=0)
            ms = jnp.mean(o_st[c] * o_st[c], axis=-1, keepdims=True)
            o_c = o_st[c] * lax.rsqrt(ms + EPS) * g_head * _silu(r_st)
            gla_rows.append(jnp.concatenate([o_c[hh * chunk:(hh + 1) * chunk] for hh in range(N_HEADS)], axis=1))
            yield
        queue_out_proj(s, "gla", jnp.concatenate(gla_rows, axis=0).astype(BF16), 0)

    for s in range(n_sub):
        ys[s] = {}
        if s + 1 < n_sub:
            queue_in_proj(s + 1)
        for n_step, _ in enumerate(stage(s)):
            emit(1)
            for _ in range(1 if n_step < SIDE_DOUBLE_FROM_STEP else 2):
                side_step()
        if s > 0:
            finish(s - 1)
        emit(len(pending) - (d_out // _W_OUT_PIECE))

    emit(len(pending))
    finish(n_sub - 1)
    if side is not None:
        for _ in side:
            pass

    for b in range(nb):
        stn_ref[b] = state[b]
        ctn_ref[b] = tail[b]
    return functools.reduce(jnp.minimum, g_mins)


def _ffn_mix_body(x0_ref, xn_ref, s0_ref, c0_ref, n1_ref, wup_ref, wdn_ref, n2_ref, win_a_ref, win_b_ref, win_c_ref,
                  wa2_ref, ba_ref, gh_ref, wc_ref, wout_ref, ltri_ref, cmask_ref, o_ref, sout_ref, cout_ref,
                  st_ref, ct_ref, stn_ref, ctn_ref, row_ref, x1_ref,
                  *, nb, tt, tps, n_tiles, chunk, sub_rows, bcast_state):
    i = pl.program_id(0)
    t = lax.rem(i, tps)
    slot = lax.rem(i, 2)
    rows = nb * tt
    d = x0_ref.shape[-1]

    @pl.when(t == 0)
    def _load_state():
        for b in range(nb):
            sb = 0 if bcast_state else b
            st_ref[b] = s0_ref[sb].reshape(D_K, DV).T
            ct_ref[b] = c0_ref[sb]

    @pl.when(i == 0)
    def _first_ffn():
        out = []
        for _ in _ffn_items(x0_ref[...].reshape(rows, d), n1_ref, wup_ref, wdn_ref, out):
            pass
        x1_ref[0] = out[0]

    tile = functools.partial(
        _mix_tile, x1_ref.at[slot], n2_ref, (win_a_ref, win_b_ref, win_c_ref), wa2_ref, ba_ref, gh_ref, wc_ref,
        wout_ref, ltri_ref, cmask_ref, o_ref, st_ref, ct_ref, stn_ref, ctn_ref, row_ref, nb=nb, tt=tt, chunk=chunk, sub_rows=sub_rows)
    if n_tiles > 1:
        next_x1 = []
        g_min = tile(direct=False,
                     side=_ffn_items(xn_ref[...].reshape(rows, d), n1_ref, wup_ref, wdn_ref, next_x1))
        x1_ref[1 - slot] = next_x1[0]
    else:
        g_min = tile(direct=False)

    @pl.when(g_min < SAFE_LOG_DECAY)
    def _redo_without_factorisation():
        tile(direct=True)

    for b in range(nb):
        st_ref[b] = stn_ref[b]
        ct_ref[b] = ctn_ref[b]

    @pl.when(t == tps - 1)
    def _store_state():
        for b in range(nb):
            sout_ref[b] = stn_ref[b].T.reshape(N_HEADS, DK, DV)
            cout_ref[b] = ctn_ref[b]


def _ffn_mix(x, s0, c0, n1, wup_p, wdn_p, n2, win_a, win_b, win_c, wa2_p, ba, gh, wc, wout, *, nb, tt):
    bsz, seq, d = x.shape
    nb = min(nb, bsz)
    tt = min(tt, seq)
    assert bsz % nb == 0 and seq % tt == 0
    chunk = min(GLA_CHUNK, tt)
    sub_rows = min(MIX_SUB_ROWS, nb * tt)
    assert tt % chunk == 0 and (nb * tt) % sub_rows == 0
    assert (nb == 1 and tt % sub_rows == 0) or sub_rows % tt == 0
    bcast_state = s0.shape[0] != bsz
    assert not bcast_state or (s0.shape[0] == 1 and c0.shape[0] == 1)
    nbs = 1 if bcast_state else nb
    tps = seq // tt
    n_tiles = (bsz // nb) * tps
    state_idx = (lambda i: (0, 0, 0, 0)) if bcast_state else (lambda i: (i // tps, 0, 0, 0))
    tail_idx = (lambda i: (0, 0, 0)) if bcast_state else (lambda i: (i // tps, 0, 0))

    def nxt(i):
        return jnp.minimum(i + 1, n_tiles - 1)

    tri = np.tril(np.ones((chunk, chunk), np.float32))
    ltri = jnp.asarray(np.concatenate([tri, tri, tri], axis=1), dtype=BF16)
    cmask = jnp.asarray(np.tile(tri, (1, N_HEADS)), dtype=F32)

    body = functools.partial(_ffn_mix_body, nb=nb, tt=tt, tps=tps, n_tiles=n_tiles, chunk=chunk, sub_rows=sub_rows,
                             bcast_state=bcast_state)
    consts = (n1, wup_p, wdn_p, n2, win_a, win_b, win_c, wa2_p, ba, gh, wc, wout, ltri, cmask)
    return pl.pallas_call(
        body,
        grid=(n_tiles,),
        in_specs=[
            pl.BlockSpec((nb, tt, d), lambda i: (0, 0, 0), pipeline_mode=pl.Buffered(1)),
            pl.BlockSpec((nb, tt, d), lambda i: (nxt(i) // tps, nxt(i) % tps, 0)),
            pl.BlockSpec((nbs, N_HEADS, DK, DV), state_idx),
            pl.BlockSpec((nbs, SUBLANES, D_CONV), tail_idx),
        ] + [_const_spec(c.shape) for c in consts],
        out_specs=[
            pl.BlockSpec((nb, tt, d), lambda i: (i // tps, i % tps, 0)),
            pl.BlockSpec((nb, N_HEADS, DK, DV), lambda i: (i // tps, 0, 0, 0)),
            pl.BlockSpec((nb, SUBLANES, D_CONV), lambda i: (i // tps, 0, 0)),
        ],
        out_shape=[
            jax.ShapeDtypeStruct((bsz, seq, d), x.dtype),
            jax.ShapeDtypeStruct((bsz, N_HEADS, DK, DV), F32),
            jax.ShapeDtypeStruct((bsz, SUBLANES, D_CONV), F32),
        ],
        scratch_shapes=[
            pltpu.VMEM((nb, DV, D_K), F32),
            pltpu.VMEM((nb, SUBLANES, D_CONV), F32),
            pltpu.VMEM((nb, DV, D_K), F32),
            pltpu.VMEM((nb, SUBLANES, D_CONV), F32),
            pltpu.VMEM((2, chunk, D_K), F32),
            pltpu.VMEM((2, nb * tt, d), F32),
        ],
        compiler_params=pltpu.CompilerParams(
            dimension_semantics=("arbitrary",), vmem_limit_bytes=VMEM_LIMIT_BYTES),
        name="ffn_mix",
    )(x, x, s0, c0, *consts)


def _pack_mix(w_in, w_a2):
    lo = 2 * D_K + 2 * D_GLA
    hi = lo + GATE_RANK
    pad = LANES - GATE_RANK
    win_c = jnp.pad(w_in[:, lo:hi], ((0, 0), (0, pad))).astype(BF16)
    wa2_p = jnp.pad(w_a2, ((0, pad), (0, 0))).astype(BF16)
    return w_in[:, :lo].astype(BF16), w_in[:, hi:].astype(BF16), win_c, wa2_p


def kernel(x_prompt, x_sample, state_gla, cache_conv, meta, norm_ffn1, w_up1, w_down1, norm_mix, w_in, w_a2, b_a, g_head, w_conv, w_out, norm_ffn2, w_up2, w_down2, norm_final):
    bp, seq, d = x_prompt.shape
    bs, seq_s, _ = x_sample.shape
    depth = w_in.shape[0]
    dt = x_prompt.dtype

    xm = meta.astype(dt)[None]
    xp, xs = x_prompt, x_sample
    sm = jnp.zeros((1, N_HEADS, DK, DV), F32)
    cm = jnp.zeros((1, SUBLANES, D_CONV), dt)
    nf = norm_final.reshape(1, d)
    sp_l, cp_l, ss_l, cs_l = [], [], [], []
    for i in range(depth):
        last = i == depth - 1
        wup1_p, wdn1_p = w_up1[i].astype(BF16), w_down1[i].astype(BF16)
        wup2_p, wdn2_p = w_up2[i].astype(BF16), w_down2[i].astype(BF16)
        win_a, win_b, win_c, wa2_p = _pack_mix(w_in[i], w_a2[i])
        lw = (norm_ffn1[i].reshape(1, d), wup1_p, wdn1_p,
              norm_mix[i].reshape(1, d), win_a, win_b, win_c, wa2_p, b_a[i].reshape(1, D_K), g_head[i].reshape(1, DV),
              w_conv[i], w_out[i].astype(BF16))
        n3 = norm_ffn2[i].reshape(1, d)
        cs0 = jnp.pad(cache_conv[i].astype(dt), ((0, 0), (SUBLANES - (CONV_W - 1), 0), (0, 0)))

        def ffn2(x):
            return _ffn(x.reshape(-1, d), n3, wup2_p, wdn2_p, nf, final_norm=last, rows=TILE_ROWS).reshape(x.shape)

        xm, sm_new, cm_new = _ffn_mix(xm, sm, cm, *lw, nb=1, tt=meta.shape[0])
        xp, sp, cp = _ffn_mix(xp, sm_new, cm_new, *lw, nb=1, tt=TILE_ROWS)
        xs, ss, cs = _ffn_mix(xs, state_gla[i].astype(F32), cs0, *lw, nb=SAMPLE_TILE_SEQS, tt=seq_s)
        xp = ffn2(xp)
        xs = ffn2(xs)
        if not last:
            xm = ffn2(xm)
        sp_l.append(sp.astype(state_gla.dtype))
        cp_l.append(cp[:, SUBLANES - (CONV_W - 1):].astype(cache_conv.dtype))
        ss_l.append(ss.astype(state_gla.dtype))
        cs_l.append(cs[:, SUBLANES - (CONV_W - 1):].astype(cache_conv.dtype))
    return (xp, xs, jnp.stack(sp_l), jnp.stack(cp_l), jnp.stack(ss_l), jnp.stack(cs_l))
```
